```python
import jax, jax.numpy as jnp
from jax import lax
import numpy as np

D_MODEL = 1024
BATCH = 4
SEQ = 4096
DEPTH = 1
DEC_BATCH = 128
DEC_SEQ = 8
PAST_LEN = 8192
PAGE_SIZE = 128

N_HEADS = 8
Q_LORA = 384
KV_LORA = 256
NOPE_DIM = 128
ROPE_DIM = 64
V_DIM = D_MODEL // N_HEADS
D_CONV = D_MODEL
CONV_W = 3
D_FF = 4 * D_MODEL
Q_BLOCK = 128
ROPE_THETA = 10000.0
EPS = 1e-6
ATTN_SCALE = (NOPE_DIM + ROPE_DIM) ** -0.5
IN_SPLITS = (Q_LORA, Q_LORA + KV_LORA, Q_LORA + KV_LORA + ROPE_DIM,
             Q_LORA + KV_LORA + ROPE_DIM + D_CONV,
             Q_LORA + KV_LORA + ROPE_DIM + 2 * D_CONV,
             Q_LORA + KV_LORA + ROPE_DIM + 3 * D_CONV)
D_IN = Q_LORA + KV_LORA + ROPE_DIM + 3 * D_CONV + 2 * D_MODEL

kernel_name = "hybrid_mla_shortconv_adaln_decoder_step"


def rmsnorm(x, g):
    xf = x.astype(jnp.float32)
    r = lax.rsqrt(jnp.mean(xf * xf, axis=-1, keepdims=True) + EPS)
    return (xf * r).astype(x.dtype) * g


def rope_angles(pos):
    inv = ROPE_THETA ** (-jnp.arange(0, ROPE_DIM, 2, dtype=jnp.float32) / ROPE_DIM)
    ang = pos.astype(jnp.float32)[:, None] * inv[None, :]
    return jnp.cos(ang), jnp.sin(ang)


def apply_rope(x, cos, sin):
    shape = (1, cos.shape[0]) + (1,) * (x.ndim - 3) + (cos.shape[1],)
    cos = cos.reshape(shape).astype(x.dtype)
    sin = sin.reshape(shape).astype(x.dtype)
    x1, x2 = jnp.split(x, 2, axis=-1)
    return jnp.concatenate([x1 * cos - x2 * sin, x2 * cos + x1 * sin], axis=-1)


def mla_attend(q_lat, q_rope, q_pos, ckv, krope, k_pos):
    s = jnp.einsum('bthc,bsc->bhts', q_lat, ckv) + jnp.einsum('bthr,bsr->bhts', q_rope, krope)
    s = s.astype(jnp.float32) * ATTN_SCALE
    mask = k_pos[None, :] <= q_pos[:, None]
    s = jnp.where(mask[None, None], s, -jnp.inf)
    p = jax.nn.softmax(s, axis=-1).astype(ckv.dtype)
    return jnp.einsum('bhts,bsc->bthc', p, ckv)


def prompt_attend(q_lat, q_rope, ckv, krope, pos):
    b, s = q_lat.shape[:2]
    nb = s // Q_BLOCK
    ql = q_lat.reshape(b, nb, Q_BLOCK, N_HEADS, KV_LORA).transpose(1, 0, 2, 3, 4)
    qr = q_rope.reshape(b, nb, Q_BLOCK, N_HEADS, ROPE_DIM).transpose(1, 0, 2, 3, 4)
    qp = pos.reshape(nb, Q_BLOCK)
    out = lax.map(lambda a: mla_attend(a[0], a[1], a[2], ckv, krope, pos), (ql, qr, qp))
    return out.transpose(1, 0, 2, 3, 4).reshape(b, s, N_HEADS, KV_LORA)


def short_conv(v, buf, conv_w):
    full = jnp.concatenate([buf, v], axis=1)
    t = v.shape[1]
    z = conv_w[0] * full[:, 0:t]
    for k in range(1, CONV_W):
        z = z + conv_w[k] * full[:, k:k + t]
    return z, full[:, full.shape[1] - (CONV_W - 1):]


def decoder_layer(x, c, pos, conv_buf, past_ckv, past_krope, past_pos, p):
    b, t, _ = x.shape
    mod = jnp.einsum('bd,de->be', jax.nn.silu(c), p['w_ada']) + p['b_ada']
    sh1, sc1, g1, sh2, sc2, g2 = [m[:, None, :] for m in jnp.split(mod, 6, axis=-1)]
    u = rmsnorm(x, p['g_attn']) * (1 + sc1) + sh1
    proj = jnp.einsum('btd,de->bte', u, p['w_in'])
    q_a, ckv, kr, b_g, c_g, xin, gates = jnp.split(proj, IN_SPLITS, axis=-1)
    q = jnp.einsum('btq,qe->bte', rmsnorm(q_a, p['g_q']), p['w_q_b'])
    q = q.reshape(b, t, N_HEADS, NOPE_DIM + ROPE_DIM)
    q_nope, q_rope = q[..., :NOPE_DIM], q[..., NOPE_DIM:]
    cos, sin = rope_angles(pos)
    q_rope = apply_rope(q_rope, cos, sin)
    ckv = rmsnorm(ckv, p['g_kv'])
    kr = apply_rope(kr, cos, sin)
    w_kvb = p['w_kv_b'].reshape(KV_LORA, N_HEADS, NOPE_DIM + V_DIM)
    w_uk, w_uv = w_kvb[..., :NOPE_DIM], w_kvb[..., NOPE_DIM:]
    q_lat = jnp.einsum('bthn,chn->bthc', q_nope, w_uk)
    if past_ckv is None:
        o_lat = prompt_attend(q_lat, q_rope, ckv, kr, pos)
    else:
        keys_ckv = jnp.concatenate([past_ckv, ckv], axis=1)
        keys_kr = jnp.concatenate([past_krope, kr], axis=1)
        k_pos = jnp.concatenate([past_pos, pos])
        o_lat = mla_attend(q_lat, q_rope, pos, keys_ckv, keys_kr, k_pos)
    attn = jnp.einsum('bthc,chv->bthv', o_lat, w_uv).reshape(b, t, N_HEADS * V_DIM)
    z, new_buf = short_conv(c_g * xin, conv_buf, p['conv_w'])
    conv_out = b_g * z
    g_a, g_b = jnp.split(jax.nn.sigmoid(gates), 2, axis=-1)
    merged = g_a * attn + g_b * conv_out
    x = x + g1 * jnp.einsum('btd,de->bte', merged, p['w_o'])
    u2 = rmsnorm(x, p['g_mlp']) * (1 + sc2) + sh2
    hdn = jnp.square(jax.nn.relu(jnp.einsum('btd,df->btf', u2, p['w_1'])))
    x = x + g2 * jnp.einsum('btf,fd->btd', hdn, p['w_2'])
    return x, ckv, kr, new_buf


def setup_inputs(seed: int = 0) -> dict:
    key = jax.random.key(seed)
    ks = jax.random.split(key, 24)
    n_pages = PAST_LEN // PAGE_SIZE
    n_used = DEC_BATCH * n_pages
    n_pool = (n_used * 5) // 4
    f32 = jnp.float32
    nrm = lambda k, s, sc: jax.random.normal(k, s, f32) * sc
    page_table = jax.random.permutation(ks[0], n_pool)[:n_used].reshape(DEC_BATCH, n_pages).astype(jnp.int32)
    return {
        "x_prompt": nrm(ks[1], (BATCH, SEQ, D_MODEL), 1.0),
        "x_sample": nrm(ks[2], (DEC_BATCH, DEC_SEQ, D_MODEL), 1.0),
        "cache_ckv": nrm(ks[3], (DEPTH, n_pool, PAGE_SIZE, KV_LORA), 1.0),
        "cache_krope": nrm(ks[4], (DEPTH, n_pool, PAGE_SIZE, ROPE_DIM), 1.0),
        "state_conv": nrm(ks[5], (DEPTH, DEC_BATCH, CONV_W - 1, D_CONV), 0.5),
        "page_table": page_table,
        "c_prompt": nrm(ks[6], (BATCH, D_MODEL), 1.0),
        "c_sample": nrm(ks[7], (DEC_BATCH, D_MODEL), 1.0),
        "w_ada": nrm(ks[8], (DEPTH, D_MODEL, 6 * D_MODEL), 0.5 * D_MODEL ** -0.5),
        "b_ada": nrm(ks[9], (DEPTH, 6 * D_MODEL), 0.02),
        "g_attn": 1.0 + nrm(ks[10], (DEPTH, D_MODEL), 0.02),
        "w_in": nrm(ks[11], (DEPTH, D_MODEL, D_IN), D_MODEL ** -0.5),
        "g_q": 1.0 + nrm(ks[12], (DEPTH, Q_LORA), 0.02),
        "w_q_b": nrm(ks[13], (DEPTH, Q_LORA, N_HEADS * (NOPE_DIM + ROPE_DIM)), Q_LORA ** -0.5),
        "g_kv": 1.0 + nrm(ks[14], (DEPTH, KV_LORA), 0.02),
        "w_kv_b": nrm(ks[15], (DEPTH, KV_LORA, N_HEADS * (NOPE_DIM + V_DIM)), KV_LORA ** -0.5),
        "conv_w": nrm(ks[16], (DEPTH, CONV_W, D_CONV), CONV_W ** -0.5),
        "w_o": nrm(ks[17], (DEPTH, D_MODEL, D_MODEL), D_MODEL ** -0.5),
        "g_mlp": 1.0 + nrm(ks[18], (DEPTH, D_MODEL), 0.02),
        "w_1": nrm(ks[19], (DEPTH, D_MODEL, D_FF), D_MODEL ** -0.5),
        "w_2": nrm(ks[20], (DEPTH, D_FF, D_MODEL), D_FF ** -0.5),
        "g_final": 1.0 + nrm(ks[21], (D_MODEL,), 0.02),
    }


def reference(x_prompt, x_sample, cache_ckv, cache_krope, state_conv, page_table, c_prompt, c_sample,
              w_ada, b_ada, g_attn, w_in, g_q, w_q_b, g_kv, w_kv_b, conv_w, w_o, g_mlp, w_1, w_2, g_final):
    seq = x_prompt.shape[1]
    dec_b, dec_t = x_sample.shape[:2]
    past_len = page_table.shape[1] * PAGE_SIZE
    pos_p = jnp.arange(seq, dtype=jnp.int32)
    pos_s = past_len + jnp.arange(dec_t, dtype=jnp.int32)
    past_pos = jnp.arange(past_len, dtype=jnp.int32)
    hp, hs = x_prompt, x_sample
    ckv_p, kr_p, cv_p, ckv_s, kr_s, cv_s = [], [], [], [], [], []
    for l in range(DEPTH):
        p = {"w_ada": w_ada[l], "b_ada": b_ada[l], "g_attn": g_attn[l], "w_in": w_in[l],
             "g_q": g_q[l], "w_q_b": w_q_b[l], "g_kv": g_kv[l], "w_kv_b": w_kv_b[l],
             "conv_w": conv_w[l], "w_o": w_o[l], "g_mlp": g_mlp[l], "w_1": w_1[l], "w_2": w_2[l]}
        buf0 = jnp.zeros((hp.shape[0], CONV_W - 1, D_CONV), hp.dtype)
        hp, a, bq, cq = decoder_layer(hp, c_prompt, pos_p, buf0, None, None, None, p)
        ckv_p.append(a); kr_p.append(bq); cv_p.append(cq)
        past_ckv = cache_ckv[l][page_table].reshape(dec_b, past_len, KV_LORA)
        past_kr = cache_krope[l][page_table].reshape(dec_b, past_len, ROPE_DIM)
        hs, a, bq, cq = decoder_layer(hs, c_sample, pos_s, state_conv[l], past_ckv, past_kr, past_pos, p)
        ckv_s.append(a); kr_s.append(bq); cv_s.append(cq)
    y_prompt = rmsnorm(hp, g_final)
    y_sample = rmsnorm(hs, g_final)
    return (y_prompt, y_sample, jnp.stack(ckv_p), jnp.stack(kr_p), jnp.stack(cv_p),
            jnp.stack(ckv_s), jnp.stack(kr_s), jnp.stack(cv_s))
```

```python
import functools

import jax
import jax.numpy as jnp
from jax import lax
from jax.experimental import pallas as pl
from jax.experimental.pallas import tpu as pltpu

D_MODEL = 1024
N_HEADS = 8
Q_LORA = 384
KV_LORA = 256
NOPE_DIM = 128
ROPE_DIM = 64
V_DIM = D_MODEL // N_HEADS
D_CONV = D_MODEL
CONV_W = 3
D_FF = 4 * D_MODEL
PAGE_SIZE = 128
ROPE_THETA = 10000.0
EPS = 1e-6
ATTN_SCALE = (NOPE_DIM + ROPE_DIM) ** -0.5

LANES = 128
QK_DIM = KV_LORA + LANES
A_COLS = Q_LORA + KV_LORA + LANES
NEG_BIG = -1e30
VMEM_LIMIT = 56 * 1024 * 1024

F32 = jnp.float32
BF16 = jnp.bfloat16


def _const_spec(shape):
    zeros = (0,) * len(shape)
    return pl.BlockSpec(shape, lambda *_: zeros, pipeline_mode=pl.Buffered(1))


def _dot(a, b):
    return jnp.dot(a, b, preferred_element_type=F32)


def _dot_nt(a, b):
    return lax.dot_general(a, b, (((1,), (1,)), ((), ())), preferred_element_type=F32)


def _rms(x):
    return x * lax.rsqrt(jnp.mean(x * x, axis=-1, keepdims=True) + EPS)


def _ada_kernel(c_ref, w_ref, b_ref, o_ref):
    c = c_ref[...]
    a = (c * jax.nn.sigmoid(c)).astype(BF16)
    o_ref[...] = _dot(a, w_ref[...].astype(BF16)) + b_ref[...]


def _ada(c_all, w_ada, b_ada):
    rows = c_all.shape[0]
    n_out = w_ada.shape[1]
    bn = D_MODEL
    return pl.pallas_call(
        _ada_kernel,
        out_shape=jax.ShapeDtypeStruct((rows, n_out), F32),
        grid=(n_out // bn,),
        in_specs=[pl.BlockSpec((rows, D_MODEL), lambda j: (0, 0)),
                  pl.BlockSpec((D_MODEL, bn), lambda j: (0, j)),
                  pl.BlockSpec((1, bn), lambda j: (0, j))],
        out_specs=pl.BlockSpec((rows, bn), lambda j: (0, j)),
        compiler_params=pltpu.CompilerParams(dimension_semantics=("arbitrary",),
                                             vmem_limit_bytes=VMEM_LIMIT),
        name="ada",
    )(c_all, w_ada, b_ada)


def _modulated(x_ref, sc_ref, sh_ref, g_ref):
    u = _rms(x_ref[...]) * g_ref[...] * (1.0 + sc_ref[...]) + sh_ref[...]
    sb, tb, d = u.shape
    return u.reshape(sb * tb, d).astype(BF16)


def _rope_pair(chunk, cs):
    t = chunk * cs
    return t + pltpu.roll(t, LANES // 2, axis=2)


def _pre_kernel(x_ref, sc_ref, sh_ref, gattn_ref, wa_ref, gq_ref, wq_ref, gkv_ref, wuk_ref, cs_ref,
                qcat_ref, ckv_ref, kr_ref, *maybe_kcat_ref, sb, tb):
    m = sb * tb
    u = _modulated(x_ref, sc_ref, sh_ref, gattn_ref)
    pa = _dot(u, wa_ref[...])
    cs = cs_ref[...]

    ckv = _rms(pa[:, Q_LORA:Q_LORA + KV_LORA]) * gkv_ref[...]
    ckv_ref[...] = ckv.reshape(sb, tb, KV_LORA)
    kr2 = _rope_pair(pa[:, Q_LORA + KV_LORA:].reshape(sb, tb, LANES), cs)
    kr_ref[...] = kr2[:, :, :ROPE_DIM]
    if maybe_kcat_ref:
        kcat_ref, = maybe_kcat_ref
        lane = lax.broadcasted_iota(jnp.int32, kr2.shape, 2)
        kcat_ref[:, :, 0:KV_LORA] = ckv.reshape(sb, tb, KV_LORA).astype(kcat_ref.dtype)
        kcat_ref[:, :, KV_LORA:QK_DIM] = jnp.where(lane < ROPE_DIM, kr2, 0.0).astype(kcat_ref.dtype)

    qn = (_rms(pa[:, :Q_LORA]) * gq_ref[...]).astype(BF16)
    q = _dot(qn, wq_ref[...])
    nope_cols = N_HEADS * NOPE_DIM
    for h in range(N_HEADS):
        q_nope = q[:, h * NOPE_DIM:(h + 1) * NOPE_DIM].astype(BF16)
        q_lat = _dot(q_nope, wuk_ref[h]) * ATTN_SCALE
        q_rope = q[:, nope_cols + h * LANES:nope_cols + (h + 1) * LANES].reshape(sb, tb, LANES)
        q_rope = _rope_pair(q_rope, cs) * ATTN_SCALE
        qcat_ref[h, :, :, 0:KV_LORA] = q_lat.reshape(sb, tb, KV_LORA).astype(qcat_ref.dtype)
        qcat_ref[h, :, :, KV_LORA:QK_DIM] = q_rope.astype(qcat_ref.dtype)


def _pre(x, mod, g_attn, w_a, g_q, w_q, g_kv, w_uk, cs, *, sb, tb, q_dtype, with_kcat):
    s_tot, t_tot, _ = x.shape
    grid = (s_tot // sb, t_tot // tb)
    tile = lambda w: pl.BlockSpec((sb, tb, w), lambda i, j: (i, j, 0))
    mod_spec = lambda chunk: pl.BlockSpec((sb, 1, D_MODEL), lambda i, j: (i, 0, chunk))
    out_shape = [jax.ShapeDtypeStruct((N_HEADS, s_tot, t_tot, QK_DIM), q_dtype),
                 jax.ShapeDtypeStruct((s_tot, t_tot, KV_LORA), F32),
                 jax.ShapeDtypeStruct((s_tot, t_tot, ROPE_DIM), F32)]
    out_specs = [pl.BlockSpec((N_HEADS, sb, tb, QK_DIM), lambda i, j: (0, i, j, 0)),
                 tile(KV_LORA), tile(ROPE_DIM)]
    if with_kcat:
        out_shape.append(jax.ShapeDtypeStruct((s_tot, t_tot, QK_DIM), BF16))
        out_specs.append(tile(QK_DIM))
    return pl.pallas_call(
        functools.partial(_pre_kernel, sb=sb, tb=tb),
        out_shape=out_shape,
        grid=grid,
        in_specs=[tile(D_MODEL), mod_spec(1), mod_spec(0),
                  _const_spec((1, D_MODEL)), _const_spec(w_a.shape), _const_spec((1, Q_LORA)),
                  _const_spec(w_q.shape), _const_spec((1, KV_LORA)), _const_spec(w_uk.shape),
                  pl.BlockSpec((tb, LANES), lambda i, j: (j, 0))],
        out_specs=out_specs,
        compiler_params=pltpu.CompilerParams(dimension_semantics=("arbitrary", "arbitrary"),
                                             vmem_limit_bytes=VMEM_LIMIT),
        name="pre",
    )(x, mod, mod, g_attn, w_a, g_q, w_q, g_kv, w_uk, cs)


def _softmax_step(s, k_lat, m_ref, l_ref, acc_ref):
    m_prev = m_ref[...]
    m_new = jnp.maximum(m_prev, jnp.max(s, axis=-1, keepdims=True))
    alpha = jnp.exp(m_prev - m_new)
    p = jnp.exp(s - m_new)
    l_ref[...] = alpha * l_ref[...] + jnp.sum(p, axis=-1, keepdims=True)
    acc_ref[...] = alpha * acc_ref[...] + _dot(p.astype(BF16), k_lat)
    m_ref[...] = m_new


def _attn_prompt_kernel(q_ref, k_ref, wuv_ref, o_ref, m_ref, l_ref, acc_ref, *, tq):
    i = pl.program_id(1)
    rows = N_HEADS * tq
    q = q_ref[...].reshape(rows, QK_DIM)
    m_ref[...] = jnp.full(m_ref.shape, NEG_BIG, F32)
    l_ref[...] = jnp.zeros(l_ref.shape, F32)
    acc_ref[...] = jnp.zeros(acc_ref.shape, F32)

    def full_chunk(j, carry):
        k = k_ref[0, pl.ds(pl.multiple_of(j * tq, tq), tq), :]
        _softmax_step(_dot_nt(q, k), k[:, :KV_LORA], m_ref, l_ref, acc_ref)
        return carry

    lax.fori_loop(0, i, full_chunk, 0)

    k = k_ref[0, pl.ds(pl.multiple_of(i * tq, tq), tq), :]
    s = _dot_nt(q, k)
    q_tok = lax.broadcasted_iota(jnp.int32, (N_HEADS, tq, tq), 1).reshape(rows, tq)
    k_tok = lax.broadcasted_iota(jnp.int32, (rows, tq), 1)
    s = jnp.where(k_tok <= q_tok, s, NEG_BIG)
    _softmax_step(s, k[:, :KV_LORA], m_ref, l_ref, acc_ref)

    o = (acc_ref[...] / l_ref[...]).astype(BF16)
    for h in range(N_HEADS):
        o_ref[0, :, h * V_DIM:(h + 1) * V_DIM] = _dot(o[h * tq:(h + 1) * tq], wuv_ref[h])


def _attn_prompt(qcat, kcat, w_uv, *, tq):
    _, b, t, _ = qcat.shape
    rows = N_HEADS * tq
    return pl.pallas_call(
        functools.partial(_attn_prompt_kernel, tq=tq),
        out_shape=jax.ShapeDtypeStruct((b, t, N_HEADS * V_DIM), F32),
        grid=(b, t // tq),
        in_specs=[pl.BlockSpec((N_HEADS, 1, tq, QK_DIM), lambda bi, i: (0, bi, i, 0)),
                  pl.BlockSpec((1, t, QK_DIM), lambda bi, i: (bi, 0, 0)),
                  _const_spec(w_uv.shape)],
        out_specs=pl.BlockSpec((1, tq, N_HEADS * V_DIM), lambda bi, i: (bi, i, 0)),
        scratch_shapes=[pltpu.VMEM((rows, 1), F32), pltpu.VMEM((rows, 1), F32),
                        pltpu.VMEM((rows, KV_LORA), F32)],
        compiler_params=pltpu.CompilerParams(dimension_semantics=("arbitrary", "arbitrary"),
                                             vmem_limit_bytes=VMEM_LIMIT),
        name="attn_prompt",
    )(qcat, kcat, w_uv)


def _attn_sample_kernel(pt_ref, q_ref, nk_ref, nr_ref, wuv_ref, ckv_hbm, kr_hbm, o_ref,
                        kbuf, rbuf, nkbuf, nrbuf, sem, *, n_pages, dec_t, chunk):
    seq = pl.program_id(0)
    n_seq = pl.num_programs(0)
    slot = seq % 2

    def page_copies(page, p, slot_):
        rows = pl.ds(pl.multiple_of(p * PAGE_SIZE, PAGE_SIZE), PAGE_SIZE)
        return (pltpu.make_async_copy(ckv_hbm.at[page], kbuf.at[slot_, rows], sem.at[0, slot_]),
                pltpu.make_async_copy(kr_hbm.at[page], rbuf.at[slot_, rows], sem.at[1, slot_]))

    def start_fetch(seq_, slot_):
        def body(p, carry):
            for cp in page_copies(pt_ref[seq_, p], p, slot_):
                cp.start()
            return carry
        lax.fori_loop(0, n_pages, body, 0)

    @pl.when(seq == 0)
    def _():
        start_fetch(0, 0)
        nkbuf[...] = jnp.zeros(nkbuf.shape, F32)
        nrbuf[...] = jnp.zeros(nrbuf.shape, F32)

    @pl.when(seq + 1 < n_seq)
    def _():
        start_fetch(seq + 1, 1 - slot)

    def wait_body(p, carry):
        for cp in page_copies(0, p, slot):
            cp.wait()
        return carry
    lax.fori_loop(0, n_pages, wait_body, 0)

    rows = N_HEADS * dec_t
    q = q_ref[...].reshape(rows, QK_DIM)
    q_lat = q[:, :KV_LORA].astype(BF16)
    q_rope = q[:, KV_LORA:KV_LORA + ROPE_DIM].astype(BF16)

    def update(s, k_lat, carry):
        m_prev, l_prev, acc = carry
        m_new = jnp.maximum(m_prev, jnp.max(s, axis=-1, keepdims=True))
        alpha = jnp.exp(m_prev - m_new)
        p = jnp.exp(s - m_new)
        l_new = alpha * l_prev + jnp.sum(p, axis=-1, keepdims=True)
        return m_new, l_new, alpha * acc + _dot(p.astype(BF16), k_lat)

    def past_chunk(c, carry):
        ks = pl.ds(pl.multiple_of(c * chunk, chunk), chunk)
        k_lat = kbuf[slot, ks, :].astype(BF16)
        k_rope = rbuf[slot, ks, :].astype(BF16)
        return update(_dot_nt(q_lat, k_lat) + _dot_nt(q_rope, k_rope), k_lat, carry)

    carry = (jnp.full((rows, 1), NEG_BIG, F32), jnp.zeros((rows, 1), F32), jnp.zeros((rows, KV_LORA), F32))
    carry = lax.fori_loop(0, n_pages * PAGE_SIZE // chunk, past_chunk, carry)

    nkbuf[0:dec_t, :] = nk_ref[0]
    nrbuf[0:dec_t, :] = nr_ref[0]
    k_lat = nkbuf[...].astype(BF16)
    s = _dot_nt(q_lat, k_lat) + _dot_nt(q_rope, nrbuf[...].astype(BF16))
    q_tok = lax.broadcasted_iota(jnp.int32, (N_HEADS, dec_t, PAGE_SIZE), 1).reshape(rows, PAGE_SIZE)
    k_tok = lax.broadcasted_iota(jnp.int32, (rows, PAGE_SIZE), 1)
    s = jnp.where(k_tok <= q_tok, s, NEG_BIG)
    _, l_fin, acc = update(s, k_lat, carry)

    o = (acc / l_fin).astype(BF16)
    for h in range(N_HEADS):
        o_ref[0, :, h * V_DIM:(h + 1) * V_DIM] = _dot(o[h * dec_t:(h + 1) * dec_t], wuv_ref[h])


def _attn_sample(page_table, qcat, new_ckv, new_kr, w_uv, cache_ckv, cache_kr, *, chunk):
    _, n_seq, dec_t, _ = qcat.shape
    n_pages = page_table.shape[1]
    past = n_pages * PAGE_SIZE
    grid_spec = pltpu.PrefetchScalarGridSpec(
        num_scalar_prefetch=1,
        grid=(n_seq,),
        in_specs=[pl.BlockSpec((N_HEADS, 1, dec_t, QK_DIM), lambda s, pt: (0, s, 0, 0)),
                  pl.BlockSpec((1, dec_t, KV_LORA), lambda s, pt: (s, 0, 0)),
                  pl.BlockSpec((1, dec_t, ROPE_DIM), lambda s, pt: (s, 0, 0)),
                  pl.BlockSpec(w_uv.shape, lambda s, pt: (0, 0, 0), pipeline_mode=pl.Buffered(1)),
                  pl.BlockSpec(memory_space=pl.ANY),
                  pl.BlockSpec(memory_space=pl.ANY)],
        out_specs=pl.BlockSpec((1, dec_t, N_HEADS * V_DIM), lambda s, pt: (s, 0, 0)),
        scratch_shapes=[pltpu.VMEM((2, past, KV_LORA), F32),
                        pltpu.VMEM((2, past, ROPE_DIM), F32),
                        pltpu.VMEM((PAGE_SIZE, KV_LORA), F32),
                        pltpu.VMEM((PAGE_SIZE, ROPE_DIM), F32),
                        pltpu.SemaphoreType.DMA((2, 2))],
    )
    return pl.pallas_call(
        functools.partial(_attn_sample_kernel, n_pages=n_pages, dec_t=dec_t, chunk=chunk),
        out_shape=jax.ShapeDtypeStruct((n_seq, dec_t, N_HEADS * V_DIM), F32),
        grid_spec=grid_spec,
        compiler_params=pltpu.CompilerParams(dimension_semantics=("arbitrary",),
                                             vmem_limit_bytes=VMEM_LIMIT),
        name="attn_sample",
    )(page_table, qcat, new_ckv, new_kr, w_uv, cache_ckv, cache_kr)


def _post1_kernel(x_ref, attn_ref, sc_ref, sh_ref, g1_ref, gattn_ref, wb_ref, wc_ref, cw_ref, wo_ref,
                  prev_ref, x1_ref, nconv_ref, carry_ref, *, sb, tb):
    m = sb * tb

    @pl.when(pl.program_id(1) == 0)
    def _():
        carry_ref[...] = prev_ref[...]

    u = _modulated(x_ref, sc_ref, sh_ref, gattn_ref)
    pb = _dot(u, wb_ref[...])
    v = (pb[:, D_CONV:2 * D_CONV] * pb[:, 2 * D_CONV:]).reshape(sb, tb, D_CONV)
    p0 = carry_ref[:, 0:1, :]
    p1 = carry_ref[:, 1:2, :]
    t = lax.broadcasted_iota(jnp.int32, v.shape, 1)
    r1 = pltpu.roll(v, 1, axis=1)
    r2 = pltpu.roll(v, 2, axis=1)
    v1 = jnp.where(t >= 1, r1, p1)
    v2 = jnp.where(t >= 2, r2, jnp.where(t == 1, p1, p0))
    cw = cw_ref[...]
    z = cw[0:1, :] * v2 + cw[1:2, :] * v1 + cw[2:3, :] * v
    conv_out = pb[:, :D_CONV] * z.reshape(m, D_CONV)
    tail = r2[:, 0:CONV_W - 1, :]
    carry_ref[...] = tail
    nconv_ref[...] = tail

    g = jax.nn.sigmoid(_dot(u, wc_ref[...]))
    merged = g[:, :D_MODEL] * attn_ref[...].reshape(m, D_MODEL) + g[:, D_MODEL:] * conv_out
    proj = _dot(merged.astype(BF16), wo_ref[...]).reshape(sb, tb, D_MODEL)
    x1_ref[...] = x_ref[...] + g1_ref[...] * proj


def _post1(x, attn, mod, g_attn, w_b, w_c, conv_w, w_o, prev, *, sb, tb):
    s_tot, t_tot, _ = x.shape
    tile = pl.BlockSpec((sb, tb, D_MODEL), lambda i, j: (i, j, 0))
    mod_spec = lambda chunk: pl.BlockSpec((sb, 1, D_MODEL), lambda i, j: (i, 0, chunk))
    state = pl.BlockSpec((sb, CONV_W - 1, D_CONV), lambda i, j: (i, 0, 0))
    return pl.pallas_call(
        functools.partial(_post1_kernel, sb=sb, tb=tb),
        out_shape=[jax.ShapeDtypeStruct(x.shape, F32),
                   jax.ShapeDtypeStruct((s_tot, CONV_W - 1, D_CONV), F32)],
        grid=(s_tot // sb, t_tot // tb),
        in_specs=[tile, tile, mod_spec(1), mod_spec(0), mod_spec(2),
                  _const_spec((1, D_MODEL)), _const_spec(w_b.shape), _const_spec(w_c.shape),
                  _const_spec(conv_w.shape), _const_spec(w_o.shape), state],
        out_specs=[tile, state],
        scratch_shapes=[pltpu.VMEM((sb, CONV_W - 1, D_CONV), F32)],
        compiler_params=pltpu.CompilerParams(dimension_semantics=("arbitrary", "arbitrary"),
                                             vmem_limit_bytes=VMEM_LIMIT),
        name="post1",
    )(x, attn, mod, mod, mod, g_attn, w_b, w_c, conv_w, w_o, prev)


def _post2_kernel(x1_ref, sc_ref, sh_ref, g2_ref, gmlp_ref, w1_ref, w2_ref, *rest, sb, tb, ff_chunk):
    y_ref = rest[-1]
    m = sb * tb
    u = _modulated(x1_ref, sc_ref, sh_ref, gmlp_ref)
    acc = jnp.zeros((m, D_MODEL), F32)
    for c in range(D_FF // ff_chunk):
        cols = slice(c * ff_chunk, (c + 1) * ff_chunk)
        hdn = jnp.square(jnp.maximum(_dot(u, w1_ref[:, cols]), 0.0))
        acc = acc + _dot(hdn.astype(BF16), w2_ref[cols, :])
    x2 = x1_ref[...] + g2_ref[...] * acc.reshape(sb, tb, D_MODEL)
    y_ref[...] = _rms(x2) * rest[0][...] if len(rest) == 2 else x2


def _post2(x1, mod, g_mlp, w_1, w_2, maybe_g_final, *, sb, tb, ff_chunk):
    s_tot, t_tot, _ = x1.shape
    tile = pl.BlockSpec((sb, tb, D_MODEL), lambda i, j: (i, j, 0))
    mod_spec = lambda chunk: pl.BlockSpec((sb, 1, D_MODEL), lambda i, j: (i, 0, chunk))
    return pl.pallas_call(
        functools.partial(_post2_kernel, sb=sb, tb=tb, ff_chunk=ff_chunk),
        out_shape=jax.ShapeDtypeStruct(x1.shape, F32),
        grid=(s_tot // sb, t_tot // tb),
        in_specs=[tile, mod_spec(4), mod_spec(3), mod_spec(5),
                  _const_spec((1, D_MODEL)), _const_spec(w_1.shape), _const_spec(w_2.shape)]
                 + [_const_spec((1, D_MODEL)) for _ in maybe_g_final],
        out_specs=tile,
        compiler_params=pltpu.CompilerParams(dimension_semantics=("arbitrary", "arbitrary"),
                                             vmem_limit_bytes=VMEM_LIMIT),
        name="post2",
    )(x1, mod, mod, mod, g_mlp, w_1, w_2, *maybe_g_final)


def _rope_table(pos):
    inv = ROPE_THETA ** (-jnp.arange(0, ROPE_DIM, 2, dtype=F32) / ROPE_DIM)
    ang = pos.astype(F32)[:, None] * inv[None, :]
    cos, sin = jnp.cos(ang), jnp.sin(ang)
    return jnp.concatenate([cos, cos, -sin, sin], axis=-1)


def _swap_halves(w):
    half = w.shape[-1] // 2
    return jnp.concatenate([w[..., half:], w[..., :half]], axis=-1)


def _layer_weights(w_in, w_q_b, w_kv_b, w_o, w_1, w_2):
    s0, s1, s2, s3 = Q_LORA, Q_LORA + KV_LORA, Q_LORA + KV_LORA + ROPE_DIM, Q_LORA + KV_LORA + ROPE_DIM + 3 * D_CONV
    w_kr = w_in[:, s1:s2]
    w_a = jnp.concatenate([w_in[:, :s1], w_kr, _swap_halves(w_kr)], axis=1).astype(BF16)
    w_b = w_in[:, s2:s3].astype(BF16)
    w_c = w_in[:, s3:].astype(BF16)
    wq = w_q_b.reshape(Q_LORA, N_HEADS, NOPE_DIM + ROPE_DIM)
    wq_nope = wq[:, :, :NOPE_DIM].reshape(Q_LORA, N_HEADS * NOPE_DIM)
    wq_rope = wq[:, :, NOPE_DIM:]
    wq_pair = jnp.concatenate([wq_rope, _swap_halves(wq_rope)], axis=-1).reshape(Q_LORA, N_HEADS * LANES)
    w_q = jnp.concatenate([wq_nope, wq_pair], axis=1).astype(BF16)
    wkv = w_kv_b.reshape(KV_LORA, N_HEADS, NOPE_DIM + V_DIM)
    w_uk = jnp.transpose(wkv[:, :, :NOPE_DIM], (1, 2, 0)).astype(BF16)
    w_uv = jnp.transpose(wkv[:, :, NOPE_DIM:], (1, 0, 2)).astype(BF16)
    return w_a, w_b, w_c, w_q, w_uk, w_uv, w_o.astype(BF16), w_1.astype(BF16), w_2.astype(BF16)


PROMPT_TILE = 512
ATTN_TILE = 256
SAMPLE_CHUNK = 1024
SAMPLE_SEQS = 64
FF_CHUNK = 1024


def kernel(x_prompt, x_sample, cache_ckv, cache_krope, state_conv, page_table, c_prompt, c_sample,
           w_ada, b_ada, g_attn, w_in, g_q, w_q_b, g_kv, w_kv_b, conv_w, w_o, g_mlp, w_1, w_2, g_final):
    depth = w_in.shape[0]
    batch, seq, _ = x_prompt.shape
    dec_b, dec_t, _ = x_sample.shape
    past_len = page_table.shape[1] * PAGE_SIZE
    cs_p = _rope_table(jnp.arange(seq, dtype=jnp.int32))
    cs_s = _rope_table(past_len + jnp.arange(dec_t, dtype=jnp.int32))
    n_c = batch + dec_b
    c_all = jnp.concatenate([c_prompt, c_sample, jnp.zeros((-n_c % 8, D_MODEL), F32)], axis=0)
    row = lambda g: g.reshape(1, -1)

    hp, hs = x_prompt, x_sample
    outs = [[] for _ in range(6)]
    for l in range(depth):
        w_a, w_b, w_c, w_q, w_uk, w_uv, w_o_b, w_1_b, w_2_b = _layer_weights(
            w_in[l], w_q_b[l], w_kv_b[l], w_o[l], w_1[l], w_2[l])
        mod = _ada(c_all, w_ada[l], row(b_ada[l]))
        mod_p = mod[:batch].reshape(batch, 1, -1)
        mod_s = mod[batch:n_c].reshape(dec_b, 1, -1)
        shared = (row(g_attn[l]), w_a, row(g_q[l]), w_q, row(g_kv[l]), w_uk)

        qcat, ckv_p, kr_p, kcat = _pre(hp, mod_p, *shared, cs_p, sb=1, tb=PROMPT_TILE,
                                       q_dtype=BF16, with_kcat=True)
        attn_p = _attn_prompt(qcat, kcat, w_uv, tq=ATTN_TILE)
        buf0 = jnp.zeros((batch, CONV_W - 1, D_CONV), F32)
        x1_p, conv_p = _post1(hp, attn_p, mod_p, row(g_attn[l]), w_b, w_c, conv_w[l], w_o_b, buf0,
                              sb=1, tb=PROMPT_TILE)

        qcat_s, ckv_s, kr_s = _pre(hs, mod_s, *shared, cs_s, sb=SAMPLE_SEQS, tb=dec_t,
                                   q_dtype=F32, with_kcat=False)
        attn_s = _attn_sample(page_table, qcat_s, ckv_s, kr_s, w_uv, cache_ckv[l], cache_krope[l],
                              chunk=SAMPLE_CHUNK)
        x1_s, conv_s = _post1(hs, attn_s, mod_s, row(g_attn[l]), w_b, w_c, conv_w[l], w_o_b, state_conv[l],
                              sb=SAMPLE_SEQS, tb=dec_t)

        g_fin = (row(g_final),) if l == depth - 1 else ()
        hp = _post2(x1_p, mod_p, row(g_mlp[l]), w_1_b, w_2_b, g_fin, sb=1, tb=PROMPT_TILE, ff_chunk=FF_CHUNK)
        hs = _post2(x1_s, mod_s, row(g_mlp[l]), w_1_b, w_2_b, g_fin, sb=SAMPLE_SEQS, tb=dec_t,
                    ff_chunk=FF_CHUNK)
        for acc, val in zip(outs, (ckv_p, kr_p, conv_p, ckv_s, kr_s, conv_s)):
            acc.append(val)
    return (hp, hs) + tuple(jnp.stack(o) for o in outs)
```

```python
import functools

import jax
import jax.numpy as jnp
from jax import lax
from jax.experimental import pallas as pl
from jax.experimental.pallas import tpu as pltpu

D_MODEL = 1024
N_HEADS = 8
Q_LORA = 384
KV_LORA = 256
NOPE_DIM = 128
ROPE_DIM = 64
V_DIM = D_MODEL // N_HEADS
D_CONV = D_MODEL
CONV_W = 3
D_FF = 4 * D_MODEL
PAGE_SIZE = 128
ROPE_THETA = 10000.0
EPS = 1e-6
ATTN_SCALE = (NOPE_DIM + ROPE_DIM) ** -0.5

LANES = 128
QK_DIM = KV_LORA + LANES
NEG_BIG = -1e30
VMEM_LIMIT = 56 * 1024 * 1024

F32 = jnp.float32
BF16 = jnp.bfloat16


def _const_spec(shape):
    zeros = (0,) * len(shape)
    return pl.BlockSpec(shape, lambda *_: zeros, pipeline_mode=pl.Buffered(1))


def _dot(a, b):
    return jnp.dot(a, b, preferred_element_type=F32)


def _dot_nt(a, b):
    return lax.dot_general(a, b, (((1,), (1,)), ((), ())), preferred_element_type=F32)


def _rms(x):
    return x * lax.rsqrt(jnp.mean(x * x, axis=-1, keepdims=True) + EPS)


def _ada_kernel(c_ref, w_ref, b_ref, o_ref):
    c = c_ref[...]
    a = (c * jax.nn.sigmoid(c)).astype(BF16)
    o_ref[...] = _dot(a, w_ref[...].astype(BF16)) + b_ref[...]


def _ada(c_all, w_ada, b_ada):
    rows = c_all.shape[0]
    n_out = w_ada.shape[1]
    bn = D_MODEL
    return pl.pallas_call(
        _ada_kernel,
        out_shape=jax.ShapeDtypeStruct((rows, n_out), F32),
        grid=(n_out // bn,),
        in_specs=[pl.BlockSpec((rows, D_MODEL), lambda j: (0, 0)),
                  pl.BlockSpec((D_MODEL, bn), lambda j: (0, j)),
                  pl.BlockSpec((1, bn), lambda j: (0, j))],
        out_specs=pl.BlockSpec((rows, bn), lambda j: (0, j)),
        compiler_params=pltpu.CompilerParams(dimension_semantics=("arbitrary",),
                                             vmem_limit_bytes=VMEM_LIMIT),
        name="ada",
    )(c_all, w_ada, b_ada)


def _modulated(x_ref, sc_ref, sh_ref, g_ref):
    u = _rms(x_ref[...]) * g_ref[...] * (1.0 + sc_ref[...]) + sh_ref[...]
    sb, tb, d = u.shape
    return u.reshape(sb * tb, d).astype(BF16)


def _rope_pair(chunk, cs):
    t = chunk * cs
    return t + pltpu.roll(t, LANES // 2, axis=2)


def _latent_and_query(x_ref, sc_ref, sh_ref, gattn_ref, wa_ref, gq_ref, wqn_ref, gkv_ref, cs_ref,
                      ckv_ref, kr_ref):
    sb, tb, _ = x_ref.shape
    u = _modulated(x_ref, sc_ref, sh_ref, gattn_ref)
    pa = _dot(u, wa_ref[...])
    ckv = _rms(pa[:, Q_LORA:Q_LORA + KV_LORA]) * gkv_ref[...]
    ckv_ref[...] = ckv.reshape(sb, tb, KV_LORA)
    kr2 = _rope_pair(pa[:, Q_LORA + KV_LORA:].reshape(sb, tb, LANES), cs_ref[...])
    kr_ref[...] = kr2[:, :, :ROPE_DIM]
    qn = (_rms(pa[:, :Q_LORA]) * gq_ref[...]).astype(BF16)
    q_nope = _dot(qn, wqn_ref[...])
    return ckv, kr2, qn, q_nope


def _pre_sample_kernel(x_ref, sc_ref, sh_ref, gattn_ref, wa_ref, gq_ref, wqn_ref, gkv_ref, cs_ref,
                       wqp_ref, wuk_ref, qcat_ref, ckv_ref, kr_ref):
    sb, tb, _ = x_ref.shape
    _, _, qn, q_nope = _latent_and_query(x_ref, sc_ref, sh_ref, gattn_ref, wa_ref, gq_ref, wqn_ref,
                                         gkv_ref, cs_ref, ckv_ref, kr_ref)
    q_pair = _dot(qn, wqp_ref[...])
    for h in range(N_HEADS):
        q_lat = _dot(q_nope[:, h * NOPE_DIM:(h + 1) * NOPE_DIM].astype(BF16), wuk_ref[h]) * ATTN_SCALE
        q_rope = _rope_pair(q_pair[:, h * LANES:(h + 1) * LANES].reshape(sb, tb, LANES), cs_ref[...])
        qcat_ref[h, :, :, 0:KV_LORA] = q_lat.reshape(sb, tb, KV_LORA)
        qcat_ref[h, :, :, KV_LORA:QK_DIM] = q_rope * ATTN_SCALE


def _pre_prompt_kernel(x_ref, sc_ref, sh_ref, gattn_ref, wa_ref, gq_ref, wqn_ref, gkv_ref, cs_ref,
                       wqpT_ref, wukT_ref, csT_ref, qT_ref, ckv_ref, kr_ref, kcat_ref, kT_ref, *, tq):
    _, tb, _ = x_ref.shape
    ckv, kr2, qn, q_nope = _latent_and_query(x_ref, sc_ref, sh_ref, gattn_ref, wa_ref, gq_ref, wqn_ref,
                                             gkv_ref, cs_ref, ckv_ref, kr_ref)
    lane = lax.broadcasted_iota(jnp.int32, kr2.shape, 2)
    kcat_ref[:, :, 0:KV_LORA] = ckv.reshape(1, tb, KV_LORA).astype(BF16)
    kcat_ref[:, :, KV_LORA:QK_DIM] = jnp.where(lane < ROPE_DIM, kr2, 0.0).astype(BF16)
    ckvT = ckv.T.astype(BF16)
    csT = csT_ref[...]
    for a in range(tb // tq):
        kT_ref[0, a] = ckvT[:, a * tq:(a + 1) * tq]
    for h in range(N_HEADS):
        q_latT = _dot_nt(wukT_ref[h], q_nope[:, h * NOPE_DIM:(h + 1) * NOPE_DIM].astype(BF16))
        t = _dot_nt(wqpT_ref[h], qn) * csT
        q_ropeT = t + jnp.concatenate([t[LANES // 2:], t[:LANES // 2]], axis=0)
        for a in range(tb // tq):
            cols = slice(a * tq, (a + 1) * tq)
            qT_ref[0, a, 0:KV_LORA, h * tq:(h + 1) * tq] = (q_latT[:, cols] * ATTN_SCALE).astype(BF16)
            qT_ref[0, a, KV_LORA:QK_DIM, h * tq:(h + 1) * tq] = (q_ropeT[:, cols] * ATTN_SCALE).astype(BF16)


def _pre_common_specs(sb, tb, w_a, w_qn):
    tile = lambda w: pl.BlockSpec((sb, tb, w), lambda i, j: (i, j, 0))
    mod_spec = lambda chunk: pl.BlockSpec((sb, 1, D_MODEL), lambda i, j: (i, 0, chunk))
    in_specs = [tile(D_MODEL), mod_spec(1), mod_spec(0),
                _const_spec((1, D_MODEL)), _const_spec(w_a.shape), _const_spec((1, Q_LORA)),
                _const_spec(w_qn.shape), _const_spec((1, KV_LORA)),
                pl.BlockSpec((tb, LANES), lambda i, j: (j, 0))]
    return tile, in_specs


def _pre_sample(x, mod, g_attn, w_a, g_q, w_qn, g_kv, cs, w_qp, w_uk, *, sb):
    s_tot, tb, _ = x.shape
    tile, in_specs = _pre_common_specs(sb, tb, w_a, w_qn)
    return pl.pallas_call(
        _pre_sample_kernel,
        out_shape=[jax.ShapeDtypeStruct((N_HEADS, s_tot, tb, QK_DIM), F32),
                   jax.ShapeDtypeStruct((s_tot, tb, KV_LORA), F32),
                   jax.ShapeDtypeStruct((s_tot, tb, ROPE_DIM), F32)],
        grid=(s_tot // sb, 1),
        in_specs=in_specs + [_const_spec(w_qp.shape), _const_spec(w_uk.shape)],
        out_specs=[pl.BlockSpec((N_HEADS, sb, tb, QK_DIM), lambda i, j: (0, i, j, 0)),
                   tile(KV_LORA), tile(ROPE_DIM)],
        compiler_params=pltpu.CompilerParams(dimension_semantics=("arbitrary", "arbitrary"),
                                             vmem_limit_bytes=VMEM_LIMIT),
        name="pre_sample",
    )(x, mod, mod, g_attn, w_a, g_q, w_qn, g_kv, cs, w_qp, w_uk)


def _pre_prompt(x, mod, g_attn, w_a, g_q, w_qn, g_kv, cs, w_qpT, w_ukT, csT, *, tb, tq):
    b, t_tot, _ = x.shape
    per = tb // tq
    tile, in_specs = _pre_common_specs(1, tb, w_a, w_qn)
    return pl.pallas_call(
        functools.partial(_pre_prompt_kernel, tq=tq),
        out_shape=[jax.ShapeDtypeStruct((b, t_tot // tq, QK_DIM, N_HEADS * tq), BF16),
                   jax.ShapeDtypeStruct((b, t_tot, KV_LORA), F32),
                   jax.ShapeDtypeStruct((b, t_tot, ROPE_DIM), F32),
                   jax.ShapeDtypeStruct((b, t_tot, QK_DIM), BF16),
                   jax.ShapeDtypeStruct((b, t_tot // tq, KV_LORA, tq), BF16)],
        grid=(b, t_tot // tb),
        in_specs=in_specs + [_const_spec(w_qpT.shape), _const_spec(w_ukT.shape),
                             pl.BlockSpec((LANES, tb), lambda i, j: (0, j))],
        out_specs=[pl.BlockSpec((1, per, QK_DIM, N_HEADS * tq), lambda i, j: (i, j, 0, 0)),
                   tile(KV_LORA), tile(ROPE_DIM), tile(QK_DIM),
                   pl.BlockSpec((1, per, KV_LORA, tq), lambda i, j: (i, j, 0, 0))],
        compiler_params=pltpu.CompilerParams(dimension_semantics=("arbitrary", "arbitrary"),
                                             vmem_limit_bytes=VMEM_LIMIT),
        name="pre_prompt",
    )(x, mod, mod, g_attn, w_a, g_q, w_qn, g_kv, cs, w_qpT, w_ukT, csT)


def _attn_prompt_kernel(qT_ref, k_ref, kT_ref, wuvT_ref, o_ref, m_ref, l_ref, acc_ref, *, tq):
    i = pl.program_id(1)
    rows = N_HEADS * tq
    qT = qT_ref[0, 0]
    m_ref[...] = jnp.full(m_ref.shape, NEG_BIG, F32)
    l_ref[...] = jnp.zeros(l_ref.shape, F32)
    acc_ref[...] = jnp.zeros(acc_ref.shape, F32)

    def keys(j):
        return k_ref[0, pl.ds(pl.multiple_of(j * tq, tq), tq), :]

    def softmax_step(sT, k_latT):
        m_prev = m_ref[...]
        m_new = jnp.maximum(m_prev, jnp.max(sT, axis=0, keepdims=True))
        alpha = jnp.exp(m_prev - m_new)
        pT = jnp.exp(sT - m_new)
        l_ref[...] = alpha * l_ref[...] + jnp.sum(pT, axis=0, keepdims=True)
        acc_ref[...] = alpha * acc_ref[...] + _dot(k_latT, pT.astype(BF16))
        m_ref[...] = m_new

    def full_chunk(j, carry):
        softmax_step(_dot(keys(j), qT), kT_ref[0, j])
        return carry

    lax.fori_loop(0, i, full_chunk, 0)

    sT = _dot(keys(i), qT)
    k_tok = lax.broadcasted_iota(jnp.int32, (tq, rows), 0)
    q_tok = lax.broadcasted_iota(jnp.int32, (tq, rows), 1) & (tq - 1)
    softmax_step(jnp.where(k_tok <= q_tok, sT, NEG_BIG), kT_ref[0, i])

    oT = (acc_ref[...] / l_ref[...]).astype(BF16)
    for h in range(N_HEADS):
        attnT = _dot(wuvT_ref[h], oT[:, h * tq:(h + 1) * tq])
        o_ref[0, :, h * V_DIM:(h + 1) * V_DIM] = attnT.T


def _attn_prompt(qT, kcat, kT, w_uvT, *, tq):
    b, n_q, _, rows = qT.shape
    assert tq & (tq - 1) == 0 and rows == N_HEADS * tq
    t = n_q * tq
    return pl.pallas_call(
        functools.partial(_attn_prompt_kernel, tq=tq),
        out_shape=jax.ShapeDtypeStruct((b, t, N_HEADS * V_DIM), F32),
        grid=(b, n_q),
        in_specs=[pl.BlockSpec((1, 1, QK_DIM, rows), lambda bi, i: (bi, i, 0, 0)),
                  pl.BlockSpec((1, t, QK_DIM), lambda bi, i: (bi, 0, 0)),
                  pl.BlockSpec((1, n_q, KV_LORA, tq), lambda bi, i: (bi, 0, 0, 0)),
                  _const_spec(w_uvT.shape)],
        out_specs=pl.BlockSpec((1, tq, N_HEADS * V_DIM), lambda bi, i: (bi, i, 0)),
        scratch_shapes=[pltpu.VMEM((1, rows), F32), pltpu.VMEM((1, rows), F32),
                        pltpu.VMEM((KV_LORA, rows), F32)],
        compiler_params=pltpu.CompilerParams(dimension_semantics=("arbitrary", "arbitrary"),
                                             vmem_limit_bytes=VMEM_LIMIT),
        name="attn_prompt",
    )(qT, kcat, kT, w_uvT)


def _attn_sample_kernel(pt_ref, q_ref, nk_ref, nr_ref, wuv_ref, ckv_hbm, krT_hbm, o_ref,
                        kbuf, rbuf, nkbuf, nrbuf, sem, *, n_pages, dec_t):
    seq = pl.program_id(0)
    n_seq = pl.num_programs(0)
    slot = seq % 2

    def page_copies(page, p, slot_):
        pos = pl.ds(pl.multiple_of(p * PAGE_SIZE, PAGE_SIZE), PAGE_SIZE)
        return (pltpu.make_async_copy(ckv_hbm.at[page], kbuf.at[slot_, pos], sem.at[0, slot_]),
                pltpu.make_async_copy(krT_hbm.at[page], rbuf.at[slot_, :, pos], sem.at[1, slot_]))

    def start_fetch(seq_, slot_):
        def body(p, carry):
            for cp in page_copies(pt_ref[seq_, p], p, slot_):
                cp.start()
            return carry
        lax.fori_loop(0, n_pages, body, 0)

    @pl.when(seq == 0)
    def _():
        start_fetch(0, 0)
        nkbuf[...] = jnp.zeros(nkbuf.shape, F32)
        nrbuf[...] = jnp.zeros(nrbuf.shape, F32)

    @pl.when(seq + 1 < n_seq)
    def _():
        start_fetch(seq + 1, 1 - slot)

    def wait_body(p, carry):
        for cp in page_copies(0, p, slot):
            cp.wait()
        return carry
    lax.fori_loop(0, n_pages, wait_body, 0)

    rows = N_HEADS * dec_t
    q = q_ref[...].reshape(rows, QK_DIM)
    q_lat = q[:, :KV_LORA].astype(BF16)
    q_rope = q[:, KV_LORA:KV_LORA + ROPE_DIM].astype(BF16)

    k_lat = kbuf[slot].astype(BF16)
    s_past = _dot_nt(q_lat, k_lat) + _dot(q_rope, rbuf[slot].astype(BF16))

    nkbuf[0:dec_t, :] = nk_ref[0]
    nrbuf[0:dec_t, :] = nr_ref[0]
    n_lat = nkbuf[...].astype(BF16)
    s_new = _dot_nt(q_lat, n_lat) + _dot_nt(q_rope, nrbuf[...].astype(BF16))
    q_tok = lax.broadcasted_iota(jnp.int32, (N_HEADS, dec_t, PAGE_SIZE), 1).reshape(rows, PAGE_SIZE)
    k_tok = lax.broadcasted_iota(jnp.int32, (rows, PAGE_SIZE), 1)
    s_new = jnp.where(k_tok <= q_tok, s_new, NEG_BIG)

    m = jnp.maximum(jnp.max(s_past, axis=-1, keepdims=True), jnp.max(s_new, axis=-1, keepdims=True))
    p_past = jnp.exp(s_past - m)
    p_new = jnp.exp(s_new - m)
    l = jnp.sum(p_past, axis=-1, keepdims=True) + jnp.sum(p_new, axis=-1, keepdims=True)
    acc = _dot(p_past.astype(BF16), k_lat) + _dot(p_new.astype(BF16), n_lat)

    o = (acc / l).astype(BF16)
    for h in range(N_HEADS):
        o_ref[0, :, h * V_DIM:(h + 1) * V_DIM] = _dot(o[h * dec_t:(h + 1) * dec_t], wuv_ref[h])


def _attn_sample(page_table, qcat, new_ckv, new_kr, w_uv, cache_ckv, cache_krT):
    _, n_seq, dec_t, _ = qcat.shape
    n_pages = page_table.shape[1]
    past = n_pages * PAGE_SIZE
    grid_spec = pltpu.PrefetchScalarGridSpec(
        num_scalar_prefetch=1,
        grid=(n_seq,),
        in_specs=[pl.BlockSpec((N_HEADS, 1, dec_t, QK_DIM), lambda s, pt: (0, s, 0, 0)),
                  pl.BlockSpec((1, dec_t, KV_LORA), lambda s, pt: (s, 0, 0)),
                  pl.BlockSpec((1, dec_t, ROPE_DIM), lambda s, pt: (s, 0, 0)),
                  pl.BlockSpec(w_uv.shape, lambda s, pt: (0, 0, 0), pipeline_mode=pl.Buffered(1)),
                  pl.BlockSpec(memory_space=pl.ANY),
                  pl.BlockSpec(memory_space=pl.ANY)],
        out_specs=pl.BlockSpec((1, dec_t, N_HEADS * V_DIM), lambda s, pt: (s, 0, 0)),
        scratch_shapes=[pltpu.VMEM((2, past, KV_LORA), F32),
                        pltpu.VMEM((2, ROPE_DIM, past), F32),
                        pltpu.VMEM((PAGE_SIZE, KV_LORA), F32),
                        pltpu.VMEM((PAGE_SIZE, ROPE_DIM), F32),
                        pltpu.SemaphoreType.DMA((2, 2))],
    )
    return pl.pallas_call(
        functools.partial(_attn_sample_kernel, n_pages=n_pages, dec_t=dec_t),
        out_shape=jax.ShapeDtypeStruct((n_seq, dec_t, N_HEADS * V_DIM), F32),
        grid_spec=grid_spec,
        compiler_params=pltpu.CompilerParams(dimension_semantics=("arbitrary",),
                                             vmem_limit_bytes=VMEM_LIMIT),
        name="attn_sample",
    )(page_table, qcat, new_ckv, new_kr, w_uv, cache_ckv, cache_krT)


def _post1_kernel(x_ref, attn_ref, sc_ref, sh_ref, g1_ref, gattn_ref, wb_ref, wc_ref, cw_ref, wo_ref,
                  prev_ref, x1_ref, nconv_ref, carry_ref, *, sb, tb):
    m = sb * tb

    @pl.when(pl.program_id(1) == 0)
    def _():
        carry_ref[...] = prev_ref[...]

    u = _modulated(x_ref, sc_ref, sh_ref, gattn_ref)
    pb = _dot(u, wb_ref[...])
    v = (pb[:, D_CONV:2 * D_CONV] * pb[:, 2 * D_CONV:]).reshape(sb, tb, D_CONV)
    p0 = carry_ref[:, 0:1, :]
    p1 = carry_ref[:, 1:2, :]
    t = lax.broadcasted_iota(jnp.int32, v.shape, 1)
    r1 = pltpu.roll(v, 1, axis=1)
    r2 = pltpu.roll(v, 2, axis=1)
    v1 = jnp.where(t >= 1, r1, p1)
    v2 = jnp.where(t >= 2, r2, jnp.where(t == 1, p1, p0))
    cw = cw_ref[...]
    z = cw[0:1, :] * v2 + cw[1:2, :] * v1 + cw[2:3, :] * v
    conv_out = pb[:, :D_CONV] * z.reshape(m, D_CONV)
    tail = r2[:, 0:CONV_W - 1, :]
    carry_ref[...] = tail
    nconv_ref[...] = tail

    g = jax.nn.sigmoid(_dot(u, wc_ref[...]))
    merged = g[:, :D_MODEL] * attn_ref[...].reshape(m, D_MODEL) + g[:, D_MODEL:] * conv_out
    proj = _dot(merged.astype(BF16), wo_ref[...]).reshape(sb, tb, D_MODEL)
    x1_ref[...] = x_ref[...] + g1_ref[...] * proj


def _post1(x, attn, mod, g_attn, w_b, w_c, conv_w, w_o, prev, *, sb, tb):
    s_tot, t_tot, _ = x.shape
    tile = pl.BlockSpec((sb, tb, D_MODEL), lambda i, j: (i, j, 0))
    mod_spec = lambda chunk: pl.BlockSpec((sb, 1, D_MODEL), lambda i, j: (i, 0, chunk))
    state = pl.BlockSpec((sb, CONV_W - 1, D_CONV), lambda i, j: (i, 0, 0))
    return pl.pallas_call(
        functools.partial(_post1_kernel, sb=sb, tb=tb),
        out_shape=[jax.ShapeDtypeStruct(x.shape, F32),
                   jax.ShapeDtypeStruct((s_tot, CONV_W - 1, D_CONV), F32)],
        grid=(s_tot // sb, t_tot // tb),
        in_specs=[tile, tile, mod_spec(1), mod_spec(0), mod_spec(2),
                  _const_spec((1, D_MODEL)), _const_spec(w_b.shape), _const_spec(w_c.shape),
                  _const_spec(conv_w.shape), _const_spec(w_o.shape), state],
        out_specs=[tile, state],
        scratch_shapes=[pltpu.VMEM((sb, CONV_W - 1, D_CONV), F32)],
        compiler_params=pltpu.CompilerParams(dimension_semantics=("arbitrary", "arbitrary"),
                                             vmem_limit_bytes=VMEM_LIMIT),
        name="post1",
    )(x, attn, mod, mod, mod, g_attn, w_b, w_c, conv_w, w_o, prev)


def _post2_kernel(x1_ref, sc_ref, sh_ref, g2_ref, gmlp_ref, w1_ref, w2_ref, *rest, sb, tb, ff_chunk):
    y_ref = rest[-1]
    m = sb * tb
    u = _modulated(x1_ref, sc_ref, sh_ref, gmlp_ref)
    acc = jnp.zeros((m, D_MODEL), F32)
    for c in range(D_FF // ff_chunk):
        cols = slice(c * ff_chunk, (c + 1) * ff_chunk)
        hdn = jnp.square(jnp.maximum(_dot(u, w1_ref[:, cols]), 0.0))
        acc = acc + _dot(hdn.astype(BF16), w2_ref[cols, :])
    x2 = x1_ref[...] + g2_ref[...] * acc.reshape(sb, tb, D_MODEL)
    y_ref[...] = _rms(x2) * rest[0][...] if len(rest) == 2 else x2


def _post2(x1, mod, g_mlp, w_1, w_2, maybe_g_final, *, sb, tb, ff_chunk):
    s_tot, t_tot, _ = x1.shape
    tile = pl.BlockSpec((sb, tb, D_MODEL), lambda i, j: (i, j, 0))
    mod_spec = lambda chunk: pl.BlockSpec((sb, 1, D_MODEL), lambda i, j: (i, 0, chunk))
    return pl.pallas_call(
        functools.partial(_post2_kernel, sb=sb, tb=tb, ff_chunk=ff_chunk),
        out_shape=jax.ShapeDtypeStruct(x1.shape, F32),
        grid=(s_tot // sb, t_tot // tb),
        in_specs=[tile, mod_spec(4), mod_spec(3), mod_spec(5),
                  _const_spec((1, D_MODEL)), _const_spec(w_1.shape), _const_spec(w_2.shape)]
                 + [_const_spec((1, D_MODEL)) for _ in maybe_g_final],
        out_specs=tile,
        compiler_params=pltpu.CompilerParams(dimension_semantics=("arbitrary", "arbitrary"),
                                             vmem_limit_bytes=VMEM_LIMIT),
        name="post2",
    )(x1, mod, mod, mod, g_mlp, w_1, w_2, *maybe_g_final)


def _rope_table(pos):
    inv = ROPE_THETA ** (-jnp.arange(0, ROPE_DIM, 2, dtype=F32) / ROPE_DIM)
    ang = pos.astype(F32)[:, None] * inv[None, :]
    cos, sin = jnp.cos(ang), jnp.sin(ang)
    return jnp.concatenate([cos, cos, -sin, sin], axis=-1)


def _swap_halves(w):
    half = w.shape[-1] // 2
    return jnp.concatenate([w[..., half:], w[..., :half]], axis=-1)


def _layer_weights(w_in, w_q_b, w_kv_b, w_o, w_1, w_2):
    s1, s2 = Q_LORA + KV_LORA, Q_LORA + KV_LORA + ROPE_DIM
    s3 = s2 + 3 * D_CONV
    w_kr = w_in[:, s1:s2]
    wq = w_q_b.reshape(Q_LORA, N_HEADS, NOPE_DIM + ROPE_DIM)
    wq_rope = wq[:, :, NOPE_DIM:]
    wq_pair = jnp.concatenate([wq_rope, _swap_halves(wq_rope)], axis=-1)
    wkv = w_kv_b.reshape(KV_LORA, N_HEADS, NOPE_DIM + V_DIM)
    w_uk, w_uv = wkv[:, :, :NOPE_DIM], wkv[:, :, NOPE_DIM:]
    weights = dict(
        w_a=jnp.concatenate([w_in[:, :s1], w_kr, _swap_halves(w_kr)], axis=1),
        w_b=w_in[:, s2:s3],
        w_c=w_in[:, s3:],
        w_qn=wq[:, :, :NOPE_DIM].reshape(Q_LORA, N_HEADS * NOPE_DIM),
        w_qp=wq_pair.reshape(Q_LORA, N_HEADS * LANES),
        w_qpT=jnp.transpose(wq_pair, (1, 2, 0)),
        w_uk=jnp.transpose(w_uk, (1, 2, 0)),
        w_ukT=jnp.transpose(w_uk, (1, 0, 2)),
        w_uv=jnp.transpose(w_uv, (1, 0, 2)),
        w_uvT=jnp.transpose(w_uv, (1, 2, 0)),
        w_o=w_o, w_1=w_1, w_2=w_2)
    return {k: v.astype(BF16) for k, v in weights.items()}


PROMPT_TILE = 512
ATTN_TILE = 256
SAMPLE_SEQS = 64
FF_CHUNK = 1024


def kernel(x_prompt, x_sample, cache_ckv, cache_krope, state_conv, page_table, c_prompt, c_sample,
           w_ada, b_ada, g_attn, w_in, g_q, w_q_b, g_kv, w_kv_b, conv_w, w_o, g_mlp, w_1, w_2, g_final):
    depth = w_in.shape[0]
    batch, seq, _ = x_prompt.shape
    dec_b, dec_t, _ = x_sample.shape
    past_len = page_table.shape[1] * PAGE_SIZE
    cs_p = _rope_table(jnp.arange(seq, dtype=jnp.int32))
    cs_s = _rope_table(past_len + jnp.arange(dec_t, dtype=jnp.int32))
    n_c = batch + dec_b
    c_all = jnp.concatenate([c_prompt, c_sample, jnp.zeros((-n_c % 8, D_MODEL), F32)], axis=0)
    row = lambda g: g.reshape(1, -1)

    hp, hs = x_prompt, x_sample
    outs = [[] for _ in range(6)]
    for l in range(depth):
        w = _layer_weights(w_in[l], w_q_b[l], w_kv_b[l], w_o[l], w_1[l], w_2[l])
        mod = _ada(c_all, w_ada[l], row(b_ada[l]))
        mod_p = mod[:batch].reshape(batch, 1, -1)
        mod_s = mod[batch:n_c].reshape(dec_b, 1, -1)
        shared = (row(g_attn[l]), w["w_a"], row(g_q[l]), w["w_qn"], row(g_kv[l]))

        qT, ckv_p, kr_p, kcat, kT = _pre_prompt(hp, mod_p, *shared, cs_p, w["w_qpT"], w["w_ukT"], cs_p.T,
                                                tb=PROMPT_TILE, tq=ATTN_TILE)
        attn_p = _attn_prompt(qT, kcat, kT, w["w_uvT"], tq=ATTN_TILE)
        buf0 = jnp.zeros((batch, CONV_W - 1, D_CONV), F32)
        x1_p, conv_p = _post1(hp, attn_p, mod_p, row(g_attn[l]), w["w_b"], w["w_c"], conv_w[l], w["w_o"], buf0,
                              sb=1, tb=PROMPT_TILE)

        qcat_s, ckv_s, kr_s = _pre_sample(hs, mod_s, *shared, cs_s, w["w_qp"], w["w_uk"], sb=SAMPLE_SEQS)
        attn_s = _attn_sample(page_table, qcat_s, ckv_s, kr_s, w["w_uv"], cache_ckv[l],
                              jnp.swapaxes(cache_krope[l], 1, 2))
        x1_s, conv_s = _post1(hs, attn_s, mod_s, row(g_attn[l]), w["w_b"], w["w_c"], conv_w[l], w["w_o"],
                              state_conv[l], sb=SAMPLE_SEQS, tb=dec_t)

        g_fin = (row(g_final),) if l == depth - 1 else ()
        hp = _post2(x1_p, mod_p, row(g_mlp[l]), w["w_1"], w["w_2"], g_fin, sb=1, tb=PROMPT_TILE,
                    ff_chunk=FF_CHUNK)
        hs = _post2(x1_s, mod_s, row(g_mlp[l]), w["w_1"], w["w_2"], g_fin, sb=SAMPLE_SEQS, tb=dec_t,
                    ff_chunk=FF_CHUNK)
        for acc, val in zip(outs, (ckv_p, kr_p, conv_p, ckv_s, kr_s, conv_s)):
            acc.append(val)
    return (hp, hs) + tuple(jnp.stack(o) for o in outs)
```

```python
import functools

import jax
import jax.numpy as jnp
from jax import lax
from jax.experimental import pallas as pl
from jax.experimental.pallas import tpu as pltpu

D_MODEL = 1024
N_HEADS = 8
Q_LORA = 384
KV_LORA = 256
NOPE_DIM = 128
ROPE_DIM = 64
V_DIM = D_MODEL // N_HEADS
D_CONV = D_MODEL
CONV_W = 3
D_FF = 4 * D_MODEL
PAGE_SIZE = 128
ROPE_THETA = 10000.0
EPS = 1e-6
ATTN_SCALE = (NOPE_DIM + ROPE_DIM) ** -0.5
LOG2_E = 1.4426950408889634

LANES = 128
QK_DIM = KV_LORA + LANES
NEG_BIG = -1e30
VMEM_LIMIT = 56 * 1024 * 1024

F32 = jnp.float32
BF16 = jnp.bfloat16


def _const_spec(shape):
    zeros = (0,) * len(shape)
    return pl.BlockSpec(shape, lambda *_: zeros, pipeline_mode=pl.Buffered(1))


def _dot(a, b):
    return jnp.dot(a, b, preferred_element_type=F32)


def _dot_nt(a, b):
    return lax.dot_general(a, b, (((1,), (1,)), ((), ())), preferred_element_type=F32)


def _rms(x):
    return x * lax.rsqrt(jnp.mean(x * x, axis=-1, keepdims=True) + EPS)


def _ada_kernel(c_ref, w_ref, b_ref, o_ref):
    c = c_ref[...]
    a = (c * jax.nn.sigmoid(c)).astype(BF16)
    o_ref[...] = _dot(a, w_ref[...].astype(BF16)) + b_ref[...]


def _ada(c_all, w_ada, b_ada):
    rows = c_all.shape[0]
    n_out = w_ada.shape[1]
    bn = D_MODEL
    return pl.pallas_call(
        _ada_kernel,
        out_shape=jax.ShapeDtypeStruct((rows, n_out), F32),
        grid=(n_out // bn,),
        in_specs=[pl.BlockSpec((rows, D_MODEL), lambda j: (0, 0)),
                  pl.BlockSpec((D_MODEL, bn), lambda j: (0, j)),
                  pl.BlockSpec((1, bn), lambda j: (0, j))],
        out_specs=pl.BlockSpec((rows, bn), lambda j: (0, j)),
        compiler_params=pltpu.CompilerParams(dimension_semantics=("arbitrary",),
                                             vmem_limit_bytes=VMEM_LIMIT),
        name="ada",
    )(c_all, w_ada, b_ada)


def _modulated(x_ref, sc_ref, sh_ref, g_ref):
    u = _rms(x_ref[...]) * g_ref[...] * (1.0 + sc_ref[...]) + sh_ref[...]
    sb, tb, d = u.shape
    return u.reshape(sb * tb, d).astype(BF16)


def _rope_pair(chunk, cs):
    t = chunk * cs
    return t + pltpu.roll(t, LANES // 2, axis=2)


def _latent_and_query(x_ref, sc_ref, sh_ref, gattn_ref, wa_ref, gq_ref, wqn_ref, gkv_ref, cs_ref,
                      ckv_ref, kr_ref):
    sb, tb, _ = x_ref.shape
    u = _modulated(x_ref, sc_ref, sh_ref, gattn_ref)
    pa = _dot(u, wa_ref[...])
    ckv = _rms(pa[:, Q_LORA:Q_LORA + KV_LORA]) * gkv_ref[...]
    ckv_ref[...] = ckv.reshape(sb, tb, KV_LORA)
    kr2 = _rope_pair(pa[:, Q_LORA + KV_LORA:].reshape(sb, tb, LANES), cs_ref[...])
    kr_ref[...] = kr2[:, :, :ROPE_DIM]
    qn = (_rms(pa[:, :Q_LORA]) * gq_ref[...]).astype(BF16)
    q_nope = _dot(qn, wqn_ref[...])
    return ckv, kr2, qn, q_nope


def _pre_sample_kernel(x_ref, sc_ref, sh_ref, gattn_ref, wa_ref, gq_ref, wqn_ref, gkv_ref, cs_ref,
                       wqp_ref, wuk_ref, qcat_ref, ckv_ref, kr_ref):
    sb, tb, _ = x_ref.shape
    _, _, qn, q_nope = _latent_and_query(x_ref, sc_ref, sh_ref, gattn_ref, wa_ref, gq_ref, wqn_ref,
                                         gkv_ref, cs_ref, ckv_ref, kr_ref)
    q_pair = _dot(qn, wqp_ref[...])
    for h in range(N_HEADS):
        q_lat = _dot(q_nope[:, h * NOPE_DIM:(h + 1) * NOPE_DIM].astype(BF16), wuk_ref[h]) * ATTN_SCALE
        q_rope = _rope_pair(q_pair[:, h * LANES:(h + 1) * LANES].reshape(sb, tb, LANES), cs_ref[...])
        qcat_ref[h, :, :, 0:KV_LORA] = q_lat.reshape(sb, tb, KV_LORA)
        qcat_ref[h, :, :, KV_LORA:QK_DIM] = q_rope * ATTN_SCALE


def _pre_prompt_kernel(x_ref, sc_ref, sh_ref, gattn_ref, wa_ref, gq_ref, wqn_ref, gkv_ref, cs_ref,
                       wqpT_ref, wukT_ref, csT_ref, qT_ref, ckv_ref, kr_ref, kcat_ref, kT_ref, *, tq):
    _, tb, _ = x_ref.shape
    ckv, kr2, qn, q_nope = _latent_and_query(x_ref, sc_ref, sh_ref, gattn_ref, wa_ref, gq_ref, wqn_ref,
                                             gkv_ref, cs_ref, ckv_ref, kr_ref)
    lane = lax.broadcasted_iota(jnp.int32, kr2.shape, 2)
    kcat_ref[:, :, 0:KV_LORA] = ckv.reshape(1, tb, KV_LORA).astype(BF16)
    kcat_ref[:, :, KV_LORA:QK_DIM] = jnp.where(lane < ROPE_DIM, kr2, 0.0).astype(BF16)
    ckvT = ckv.T.astype(BF16)
    csT = csT_ref[...]
    scale = ATTN_SCALE * LOG2_E
    for a in range(tb // tq):
        kT_ref[0, a] = ckvT[:, a * tq:(a + 1) * tq]
    for h in range(N_HEADS):
        q_latT = _dot_nt(wukT_ref[h], q_nope[:, h * NOPE_DIM:(h + 1) * NOPE_DIM].astype(BF16))
        t = _dot_nt(wqpT_ref[h], qn) * csT
        q_ropeT = t + jnp.concatenate([t[LANES // 2:], t[:LANES // 2]], axis=0)
        for a in range(tb // tq):
            cols = slice(a * tq, (a + 1) * tq)
            qT_ref[0, a, 0:KV_LORA, h * tq:(h + 1) * tq] = (q_latT[:, cols] * scale).astype(BF16)
            qT_ref[0, a, KV_LORA:QK_DIM, h * tq:(h + 1) * tq] = (q_ropeT[:, cols] * scale).astype(BF16)


def _pre_common_specs(sb, tb, w_a, w_qn):
    tile = lambda w: pl.BlockSpec((sb, tb, w), lambda i, j: (i, j, 0))
    mod_spec = lambda chunk: pl.BlockSpec((sb, 1, D_MODEL), lambda i, j: (i, 0, chunk))
    in_specs = [tile(D_MODEL), mod_spec(1), mod_spec(0),
                _const_spec((1, D_MODEL)), _const_spec(w_a.shape), _const_spec((1, Q_LORA)),
                _const_spec(w_qn.shape), _const_spec((1, KV_LORA)),
                pl.BlockSpec((tb, LANES), lambda i, j: (j, 0))]
    return tile, in_specs


def _pre_sample(x, mod, g_attn, w_a, g_q, w_qn, g_kv, cs, w_qp, w_uk, *, sb):
    s_tot, tb, _ = x.shape
    tile, in_specs = _pre_common_specs(sb, tb, w_a, w_qn)
    return pl.pallas_call(
        _pre_sample_kernel,
        out_shape=[jax.ShapeDtypeStruct((N_HEADS, s_tot, tb, QK_DIM), F32),
                   jax.ShapeDtypeStruct((s_tot, tb, KV_LORA), F32),
                   jax.ShapeDtypeStruct((s_tot, tb, ROPE_DIM), F32)],
        grid=(s_tot // sb, 1),
        in_specs=in_specs + [_const_spec(w_qp.shape), _const_spec(w_uk.shape)],
        out_specs=[pl.BlockSpec((N_HEADS, sb, tb, QK_DIM), lambda i, j: (0, i, j, 0)),
                   tile(KV_LORA), tile(ROPE_DIM)],
        compiler_params=pltpu.CompilerParams(dimension_semantics=("arbitrary", "arbitrary"),
                                             vmem_limit_bytes=VMEM_LIMIT),
        name="pre_sample",
    )(x, mod, mod, g_attn, w_a, g_q, w_qn, g_kv, cs, w_qp, w_uk)


def _pre_prompt(x, mod, g_attn, w_a, g_q, w_qn, g_kv, cs, w_qpT, w_ukT, csT, *, tb, tq):
    b, t_tot, _ = x.shape
    per = tb // tq
    tile, in_specs = _pre_common_specs(1, tb, w_a, w_qn)
    return pl.pallas_call(
        functools.partial(_pre_prompt_kernel, tq=tq),
        out_shape=[jax.ShapeDtypeStruct((b, t_tot // tq, QK_DIM, N_HEADS * tq), BF16),
                   jax.ShapeDtypeStruct((b, t_tot, KV_LORA), F32),
                   jax.ShapeDtypeStruct((b, t_tot, ROPE_DIM), F32),
                   jax.ShapeDtypeStruct((b, t_tot, QK_DIM), BF16),
                   jax.ShapeDtypeStruct((b, t_tot // tq, KV_LORA, tq), BF16)],
        grid=(b, t_tot // tb),
        in_specs=in_specs + [_const_spec(w_qpT.shape), _const_spec(w_ukT.shape),
                             pl.BlockSpec((LANES, tb), lambda i, j: (0, j))],
        out_specs=[pl.BlockSpec((1, per, QK_DIM, N_HEADS * tq), lambda i, j: (i, j, 0, 0)),
                   tile(KV_LORA), tile(ROPE_DIM), tile(QK_DIM),
                   pl.BlockSpec((1, per, KV_LORA, tq), lambda i, j: (i, j, 0, 0))],
        compiler_params=pltpu.CompilerParams(dimension_semantics=("arbitrary", "arbitrary"),
                                             vmem_limit_bytes=VMEM_LIMIT),
        name="pre_prompt",
    )(x, mod, mod, g_attn, w_a, g_q, w_qn, g_kv, cs, w_qpT, w_ukT, csT)


HEAD_LOOKAHEAD = 3


def _attn_prompt_kernel(qT_ref, k_ref, kT_ref, wuvT_ref, o_ref, m_ref, l_ref, acc_ref, ahead_ref, *,
                        tq, lookahead):
    i = pl.program_id(1)
    m_ref[...] = jnp.full(m_ref.shape, NEG_BIG, F32)
    l_ref[...] = jnp.zeros(l_ref.shape, F32)
    acc_ref[...] = jnp.zeros(acc_ref.shape, F32)
    heads = [slice(h * tq, (h + 1) * tq) for h in range(N_HEADS)]

    def scores(j, h):
        keys = k_ref[0, pl.ds(pl.multiple_of(j * tq, tq), tq), :]
        return _dot(keys, qT_ref[0, 0, :, heads[h]])

    def chunk(j, visible, has_next):
        k_latT = kT_ref[0, j]
        pending = [ahead_ref[h] for h in range(lookahead)]
        for h in range(N_HEADS):
            if h + lookahead < N_HEADS:
                pending.append(scores(j, h + lookahead))
            elif has_next:
                ahead_ref[h + lookahead - N_HEADS] = scores(j + 1, h + lookahead - N_HEADS)
            sT = pending.pop(0)
            if visible is not None:
                sT = jnp.where(visible, sT, NEG_BIG)
            m_prev = m_ref[:, heads[h]]
            m_new = jnp.maximum(m_prev, jnp.max(sT, axis=0, keepdims=True))
            alpha = jnp.exp2(m_prev - m_new)
            pT = jnp.exp2(sT - m_new)
            l_ref[:, heads[h]] = alpha * l_ref[:, heads[h]] + jnp.sum(pT, axis=0, keepdims=True)
            acc_ref[:, heads[h]] = alpha * acc_ref[:, heads[h]] + _dot(k_latT, pT.astype(BF16))
            m_ref[:, heads[h]] = m_new

    for h in range(lookahead):
        ahead_ref[h] = scores(0, h)

    def full_chunk(j, carry):
        chunk(j, None, True)
        return carry

    lax.fori_loop(0, i, full_chunk, 0)

    k_tok = lax.broadcasted_iota(jnp.int32, (tq, tq), 0)
    q_tok = lax.broadcasted_iota(jnp.int32, (tq, tq), 1)
    chunk(i, k_tok <= q_tok, False)

    for h in range(N_HEADS):
        oT = (acc_ref[:, heads[h]] / l_ref[:, heads[h]]).astype(BF16)
        o_ref[0, :, h * V_DIM:(h + 1) * V_DIM] = _dot(wuvT_ref[h], oT).T


def _attn_prompt(qT, kcat, kT, w_uvT, *, tq):
    b, n_q, _, rows = qT.shape
    assert tq & (tq - 1) == 0 and rows == N_HEADS * tq
    t = n_q * tq
    return pl.pallas_call(
        functools.partial(_attn_prompt_kernel, tq=tq, lookahead=HEAD_LOOKAHEAD),
        out_shape=jax.ShapeDtypeStruct((b, t, N_HEADS * V_DIM), F32),
        grid=(b, n_q),
        in_specs=[pl.BlockSpec((1, 1, QK_DIM, rows), lambda bi, i: (bi, i, 0, 0)),
                  pl.BlockSpec((1, t, QK_DIM), lambda bi, i: (bi, 0, 0)),
                  pl.BlockSpec((1, n_q, KV_LORA, tq), lambda bi, i: (bi, 0, 0, 0)),
                  _const_spec(w_uvT.shape)],
        out_specs=pl.BlockSpec((1, tq, N_HEADS * V_DIM), lambda bi, i: (bi, i, 0)),
        scratch_shapes=[pltpu.VMEM((1, rows), F32), pltpu.VMEM((1, rows), F32),
                        pltpu.VMEM((KV_LORA, rows), F32), pltpu.VMEM((HEAD_LOOKAHEAD, tq, tq), F32)],
        compiler_params=pltpu.CompilerParams(dimension_semantics=("arbitrary", "arbitrary"),
                                             vmem_limit_bytes=VMEM_LIMIT),
        name="attn_prompt",
    )(qT, kcat, kT, w_uvT)


DMA_UNROLL = 8
SAMPLE_KEY_BLOCKS = 16
SAMPLE_LOOKAHEAD = 3


def _attn_sample_kernel(pt_ref, q_ref, nk_ref, nr_ref, wuv_ref, ckv_hbm, krT_hbm, o_ref,
                        kbuf, rbuf, nkbuf, nrbuf, sem, *, n_pages, dec_t, n_blocks, lookahead):
    seq = pl.program_id(0)
    n_seq = pl.num_programs(0)
    slot = seq % 2

    def page_copies(page, p, slot_):
        pos = pl.ds(pl.multiple_of(p * PAGE_SIZE, PAGE_SIZE), PAGE_SIZE)
        return (pltpu.make_async_copy(ckv_hbm.at[page], kbuf.at[slot_, pos], sem.at[0, slot_]),
                pltpu.make_async_copy(krT_hbm.at[page], rbuf.at[slot_, :, pos], sem.at[1, slot_]))

    def start_fetch(seq_, slot_):
        def body(p, carry):
            for cp in page_copies(pt_ref[seq_, p], p, slot_):
                cp.start()
            return carry
        lax.fori_loop(0, n_pages, body, 0, unroll=DMA_UNROLL)

    @pl.when(seq == 0)
    def _():
        start_fetch(0, 0)
        nkbuf[...] = jnp.zeros(nkbuf.shape, F32)
        nrbuf[...] = jnp.zeros(nrbuf.shape, F32)

    @pl.when(seq + 1 < n_seq)
    def _():
        start_fetch(seq + 1, 1 - slot)

    def wait_body(p, carry):
        for cp in page_copies(0, p, slot):
            cp.wait()
        return carry
    lax.fori_loop(0, n_pages, wait_body, 0, unroll=DMA_UNROLL)

    rows = N_HEADS * dec_t
    past = n_pages * PAGE_SIZE
    q = q_ref[...].reshape(rows, QK_DIM)
    q_lat = q[:, :KV_LORA].astype(BF16)
    q_rope = q[:, KV_LORA:KV_LORA + ROPE_DIM].astype(BF16)

    def partial_softmax(s, k_lat):
        m_b = jnp.max(s, axis=-1, keepdims=True)
        p = jnp.exp(s - m_b)
        return m_b, jnp.sum(p, axis=-1, keepdims=True), _dot(p.astype(BF16), k_lat)

    def block_scores(b):
        ks = slice(b * past // n_blocks, (b + 1) * past // n_blocks)
        k_lat = kbuf[slot, ks, :].astype(BF16)
        return _dot_nt(q_lat, k_lat) + _dot(q_rope, rbuf[slot, :, ks].astype(BF16)), k_lat

    pending = [block_scores(b) for b in range(lookahead)]
    parts = []
    for b in range(n_blocks):
        if b + lookahead < n_blocks:
            pending.append(block_scores(b + lookahead))
        parts.append(partial_softmax(*pending.pop(0)))

    nkbuf[0:dec_t, :] = nk_ref[0]
    nrbuf[0:dec_t, :] = nr_ref[0]
    n_lat = nkbuf[...].astype(BF16)
    s_new = _dot_nt(q_lat, n_lat) + _dot_nt(q_rope, nrbuf[...].astype(BF16))
    q_tok = lax.broadcasted_iota(jnp.int32, (N_HEADS, dec_t, PAGE_SIZE), 1).reshape(rows, PAGE_SIZE)
    k_tok = lax.broadcasted_iota(jnp.int32, (rows, PAGE_SIZE), 1)
    s_new = jnp.where(k_tok <= q_tok, s_new, NEG_BIG)
    parts.append(partial_softmax(s_new, n_lat))

    m = functools.reduce(jnp.maximum, [m_b for m_b, _, _ in parts])
    weights = [jnp.exp(m_b - m) for m_b, _, _ in parts]
    l = functools.reduce(jnp.add, [w_b * l_b for w_b, (_, l_b, _) in zip(weights, parts)])
    acc = functools.reduce(jnp.add, [w_b * acc_b for w_b, (_, _, acc_b) in zip(weights, parts)])

    o = (acc / l).astype(BF16)
    for h in range(N_HEADS):
        o_ref[0, :, h * V_DIM:(h + 1) * V_DIM] = _dot(o[h * dec_t:(h + 1) * dec_t], wuv_ref[h])


def _attn_sample(page_table, qcat, new_ckv, new_kr, w_uv, cache_ckv, cache_krT):
    _, n_seq, dec_t, _ = qcat.shape
    n_pages = page_table.shape[1]
    past = n_pages * PAGE_SIZE
    grid_spec = pltpu.PrefetchScalarGridSpec(
        num_scalar_prefetch=1,
        grid=(n_seq,),
        in_specs=[pl.BlockSpec((N_HEADS, 1, dec_t, QK_DIM), lambda s, pt: (0, s, 0, 0)),
                  pl.BlockSpec((1, dec_t, KV_LORA), lambda s, pt: (s, 0, 0)),
                  pl.BlockSpec((1, dec_t, ROPE_DIM), lambda s, pt: (s, 0, 0)),
                  pl.BlockSpec(w_uv.shape, lambda s, pt: (0, 0, 0), pipeline_mode=pl.Buffered(1)),
                  pl.BlockSpec(memory_space=pl.ANY),
                  pl.BlockSpec(memory_space=pl.ANY)],
        out_specs=pl.BlockSpec((1, dec_t, N_HEADS * V_DIM), lambda s, pt: (s, 0, 0)),
        scratch_shapes=[pltpu.VMEM((2, past, KV_LORA), F32),
                        pltpu.VMEM((2, ROPE_DIM, past), F32),
                        pltpu.VMEM((PAGE_SIZE, KV_LORA), F32),
                        pltpu.VMEM((PAGE_SIZE, ROPE_DIM), F32),
                        pltpu.SemaphoreType.DMA((2, 2))],
    )
    return pl.pallas_call(
        functools.partial(_attn_sample_kernel, n_pages=n_pages, dec_t=dec_t, n_blocks=SAMPLE_KEY_BLOCKS,
                          lookahead=SAMPLE_LOOKAHEAD),
        out_shape=jax.ShapeDtypeStruct((n_seq, dec_t, N_HEADS * V_DIM), F32),
        grid_spec=grid_spec,
        compiler_params=pltpu.CompilerParams(dimension_semantics=("arbitrary",),
                                             vmem_limit_bytes=VMEM_LIMIT),
        name="attn_sample",
    )(page_table, qcat, new_ckv, new_kr, w_uv, cache_ckv, cache_krT)


def _post1_kernel(x_ref, attn_ref, sc_ref, sh_ref, g1_ref, gattn_ref, wb_ref, wc_ref, cw_ref, wo_ref,
                  prev_ref, x1_ref, nconv_ref, carry_ref, *, sb, tb):
    m = sb * tb

    @pl.when(pl.program_id(1) == 0)
    def _():
        carry_ref[...] = prev_ref[...]

    u = _modulated(x_ref, sc_ref, sh_ref, gattn_ref)
    pb = _dot(u, wb_ref[...])
    v = (pb[:, D_CONV:2 * D_CONV] * pb[:, 2 * D_CONV:]).reshape(sb, tb, D_CONV)
    p0 = carry_ref[:, 0:1, :]
    p1 = carry_ref[:, 1:2, :]
    t = lax.broadcasted_iota(jnp.int32, v.shape, 1)
    r1 = pltpu.roll(v, 1, axis=1)
    r2 = pltpu.roll(v, 2, axis=1)
    v1 = jnp.where(t >= 1, r1, p1)
    v2 = jnp.where(t >= 2, r2, jnp.where(t == 1, p1, p0))
    cw = cw_ref[...]
    z = cw[0:1, :] * v2 + cw[1:2, :] * v1 + cw[2:3, :] * v
    conv_out = pb[:, :D_CONV] * z.reshape(m, D_CONV)
    tail = r2[:, 0:CONV_W - 1, :]
    carry_ref[...] = tail
    nconv_ref[...] = tail

    g = jax.nn.sigmoid(_dot(u, wc_ref[...]))
    merged = g[:, :D_MODEL] * attn_ref[...].reshape(m, D_MODEL) + g[:, D_MODEL:] * conv_out
    proj = _dot(merged.astype(BF16), wo_ref[...]).reshape(sb, tb, D_MODEL)
    x1_ref[...] = x_ref[...] + g1_ref[...] * proj


def _post1(x, attn, mod, g_attn, w_b, w_c, conv_w, w_o, prev, *, sb, tb):
    s_tot, t_tot, _ = x.shape
    tile = pl.BlockSpec((sb, tb, D_MODEL), lambda i, j: (i, j, 0))
    mod_spec = lambda chunk: pl.BlockSpec((sb, 1, D_MODEL), lambda i, j: (i, 0, chunk))
    state = pl.BlockSpec((sb, CONV_W - 1, D_CONV), lambda i, j: (i, 0, 0))
    return pl.pallas_call(
        functools.partial(_post1_kernel, sb=sb, tb=tb),
        out_shape=[jax.ShapeDtypeStruct(x.shape, F32),
                   jax.ShapeDtypeStruct((s_tot, CONV_W - 1, D_CONV), F32)],
        grid=(s_tot // sb, t_tot // tb),
        in_specs=[tile, tile, mod_spec(1), mod_spec(0), mod_spec(2),
                  _const_spec((1, D_MODEL)), _const_spec(w_b.shape), _const_spec(w_c.shape),
                  _const_spec(conv_w.shape), _const_spec(w_o.shape), state],
        out_specs=[tile, state],
        scratch_shapes=[pltpu.VMEM((sb, CONV_W - 1, D_CONV), F32)],
        compiler_params=pltpu.CompilerParams(dimension_semantics=("arbitrary", "arbitrary"),
                                             vmem_limit_bytes=VMEM_LIMIT),
        name="post1",
    )(x, attn, mod, mod, mod, g_attn, w_b, w_c, conv_w, w_o, prev)


def _post2_kernel(x1_ref, sc_ref, sh_ref, g2_ref, gmlp_ref, w1_ref, w2_ref, *rest, sb, tb, ff_chunk):
    y_ref = rest[-1]
    m = sb * tb
    u = _modulated(x1_ref, sc_ref, sh_ref, gmlp_ref)
    acc = jnp.zeros((m, D_MODEL), F32)
    for c in range(D_FF // ff_chunk):
        cols = slice(c * ff_chunk, (c + 1) * ff_chunk)
        hdn = jnp.square(jnp.maximum(_dot(u, w1_ref[:, cols]), 0.0))
        acc = acc + _dot(hdn.astype(BF16), w2_ref[cols, :])
    x2 = x1_ref[...] + g2_ref[...] * acc.reshape(sb, tb, D_MODEL)
    y_ref[...] = _rms(x2) * rest[0][...] if len(rest) == 2 else x2


def _post2(x1, mod, g_mlp, w_1, w_2, maybe_g_final, *, sb, tb, ff_chunk):
    s_tot, t_tot, _ = x1.shape
    tile = pl.BlockSpec((sb, tb, D_MODEL), lambda i, j: (i, j, 0))
    mod_spec = lambda chunk: pl.BlockSpec((sb, 1, D_MODEL), lambda i, j: (i, 0, chunk))
    return pl.pallas_call(
        functools.partial(_post2_kernel, sb=sb, tb=tb, ff_chunk=ff_chunk),
        out_shape=jax.ShapeDtypeStruct(x1.shape, F32),
        grid=(s_tot // sb, t_tot // tb),
        in_specs=[tile, mod_spec(4), mod_spec(3), mod_spec(5),
                  _const_spec((1, D_MODEL)), _const_spec(w_1.shape), _const_spec(w_2.shape)]
                 + [_const_spec((1, D_MODEL)) for _ in maybe_g_final],
        out_specs=tile,
        compiler_params=pltpu.CompilerParams(dimension_semantics=("arbitrary", "arbitrary"),
                                             vmem_limit_bytes=VMEM_LIMIT),
        name="post2",
    )(x1, mod, mod, mod, g_mlp, w_1, w_2, *maybe_g_final)


def _rope_table(pos):
    inv = ROPE_THETA ** (-jnp.arange(0, ROPE_DIM, 2, dtype=F32) / ROPE_DIM)
    ang = pos.astype(F32)[:, None] * inv[None, :]
    cos, sin = jnp.cos(ang), jnp.sin(ang)
    return jnp.concatenate([cos, cos, -sin, sin], axis=-1)


def _swap_halves(w):
    half = w.shape[-1] // 2
    return jnp.concatenate([w[..., half:], w[..., :half]], axis=-1)


def _layer_weights(w_in, w_q_b, w_kv_b, w_o, w_1, w_2):
    s1, s2 = Q_LORA + KV_LORA, Q_LORA + KV_LORA + ROPE_DIM
    s3 = s2 + 3 * D_CONV
    w_kr = w_in[:, s1:s2]
    wq = w_q_b.reshape(Q_LORA, N_HEADS, NOPE_DIM + ROPE_DIM)
    wq_rope = wq[:, :, NOPE_DIM:]
    wq_pair = jnp.concatenate([wq_rope, _swap_halves(wq_rope)], axis=-1)
    wkv = w_kv_b.reshape(KV_LORA, N_HEADS, NOPE_DIM + V_DIM)
    w_uk, w_uv = wkv[:, :, :NOPE_DIM], wkv[:, :, NOPE_DIM:]
    weights = dict(
        w_a=jnp.concatenate([w_in[:, :s1], w_kr, _swap_halves(w_kr)], axis=1),
        w_b=w_in[:, s2:s3],
        w_c=w_in[:, s3:],
        w_qn=wq[:, :, :NOPE_DIM].reshape(Q_LORA, N_HEADS * NOPE_DIM),
        w_qp=wq_pair.reshape(Q_LORA, N_HEADS * LANES),
        w_qpT=jnp.transpose(wq_pair, (1, 2, 0)),
        w_uk=jnp.transpose(w_uk, (1, 2, 0)),
        w_ukT=jnp.transpose(w_uk, (1, 0, 2)),
        w_uv=jnp.transpose(w_uv, (1, 0, 2)),
        w_uvT=jnp.transpose(w_uv, (1, 2, 0)),
        w_o=w_o, w_1=w_1, w_2=w_2)
    return {k: v.astype(BF16) for k, v in weights.items()}


PROMPT_TILE = 512
ATTN_TILE = 256
SAMPLE_SEQS = 64
FF_CHUNK = 1024


def kernel(x_prompt, x_sample, cache_ckv, cache_krope, state_conv, page_table, c_prompt, c_sample,
           w_ada, b_ada, g_attn, w_in, g_q, w_q_b, g_kv, w_kv_b, conv_w, w_o, g_mlp, w_1, w_2, g_final):
    depth = w_in.shape[0]
    batch, seq, _ = x_prompt.shape
    dec_b, dec_t, _ = x_sample.shape
    past_len = page_table.shape[1] * PAGE_SIZE
    cs_p = _rope_table(jnp.arange(seq, dtype=jnp.int32))
    cs_s = _rope_table(past_len + jnp.arange(dec_t, dtype=jnp.int32))
    n_c = batch + dec_b
    c_all = jnp.concatenate([c_prompt, c_sample, jnp.zeros((-n_c % 8, D_MODEL), F32)], axis=0)
    row = lambda g: g.reshape(1, -1)

    hp, hs = x_prompt, x_sample
    outs = [[] for _ in range(6)]
    for l in range(depth):
        w = _layer_weights(w_in[l], w_q_b[l], w_kv_b[l], w_o[l], w_1[l], w_2[l])
        mod = _ada(c_all, w_ada[l], row(b_ada[l]))
        mod_p = mod[:batch].reshape(batch, 1, -1)
        mod_s = mod[batch:n_c].reshape(dec_b, 1, -1)
        shared = (row(g_attn[l]), w["w_a"], row(g_q[l]), w["w_qn"], row(g_kv[l]))

        qT, ckv_p, kr_p, kcat, kT = _pre_prompt(hp, mod_p, *shared, cs_p, w["w_qpT"], w["w_ukT"], cs_p.T,
                                                tb=PROMPT_TILE, tq=ATTN_TILE)
        attn_p = _attn_prompt(qT, kcat, kT, w["w_uvT"], tq=ATTN_TILE)
        buf0 = jnp.zeros((batch, CONV_W - 1, D_CONV), F32)
        x1_p, conv_p = _post1(hp, attn_p, mod_p, row(g_attn[l]), w["w_b"], w["w_c"], conv_w[l], w["w_o"], buf0,
                              sb=1, tb=PROMPT_TILE)

        qcat_s, ckv_s, kr_s = _pre_sample(hs, mod_s, *shared, cs_s, w["w_qp"], w["w_uk"], sb=SAMPLE_SEQS)
        attn_s = _attn_sample(page_table, qcat_s, ckv_s, kr_s, w["w_uv"], cache_ckv[l],
                              jnp.swapaxes(cache_krope[l], 1, 2))
        x1_s, conv_s = _post1(hs, attn_s, mod_s, row(g_attn[l]), w["w_b"], w["w_c"], conv_w[l], w["w_o"],
                              state_conv[l], sb=SAMPLE_SEQS, tb=dec_t)

        g_fin = (row(g_final),) if l == depth - 1 else ()
        hp = _post2(x1_p, mod_p, row(g_mlp[l]), w["w_1"], w["w_2"], g_fin, sb=1, tb=PROMPT_TILE,
                    ff_chunk=FF_CHUNK)
        hs = _post2(x1_s, mod_s, row(g_mlp[l]), w["w_1"], w["w_2"], g_fin, sb=SAMPLE_SEQS, tb=dec_t,
                    ff_chunk=FF_CHUNK)
        for acc, val in zip(outs, (ckv_p, kr_p, conv_p, ckv_s, kr_s, conv_s)):
            acc.append(val)
    return (hp, hs) + tuple(jnp.stack(o) for o in outs)
```

```python
import functools

import jax
import jax.numpy as jnp
from jax import lax
from jax.experimental import pallas as pl
from jax.experimental.pallas import tpu as pltpu

D_MODEL = 1024
N_HEADS = 8
Q_LORA = 384
KV_LORA = 256
NOPE_DIM = 128
ROPE_DIM = 64
V_DIM = D_MODEL // N_HEADS
D_CONV = D_MODEL
CONV_W = 3
D_FF = 4 * D_MODEL
PAGE_SIZE = 128
ROPE_THETA = 10000.0
EPS = 1e-6
ATTN_SCALE = (NOPE_DIM + ROPE_DIM) ** -0.5
LOG2_E = 1.4426950408889634

LANES = 128
QK_DIM = KV_LORA + LANES
NEG_BIG = -1e30
VMEM_LIMIT = 56 * 1024 * 1024

F32 = jnp.float32
BF16 = jnp.bfloat16


def _const_spec(shape):
    zeros = (0,) * len(shape)
    return pl.BlockSpec(shape, lambda *_: zeros, pipeline_mode=pl.Buffered(1))


def _dot(a, b):
    return jnp.dot(a, b, preferred_element_type=F32)


def _dot_nt(a, b):
    return lax.dot_general(a, b, (((1,), (1,)), ((), ())), preferred_element_type=F32)


def _rms(x):
    return x * lax.rsqrt(jnp.mean(x * x, axis=-1, keepdims=True) + EPS)


def _ada_kernel(c_ref, w_ref, b_ref, o_ref):
    c = c_ref[...]
    a = (c * jax.nn.sigmoid(c)).astype(BF16)
    o_ref[...] = _dot(a, w_ref[...].astype(BF16)) + b_ref[...]


def _ada(c_all, w_ada, b_ada):
    rows = c_all.shape[0]
    n_out = w_ada.shape[1]
    bn = D_MODEL
    return pl.pallas_call(
        _ada_kernel,
        out_shape=jax.ShapeDtypeStruct((rows, n_out), F32),
        grid=(n_out // bn,),
        in_specs=[pl.BlockSpec((rows, D_MODEL), lambda j: (0, 0)),
                  pl.BlockSpec((D_MODEL, bn), lambda j: (0, j)),
                  pl.BlockSpec((1, bn), lambda j: (0, j))],
        out_specs=pl.BlockSpec((rows, bn), lambda j: (0, j)),
        compiler_params=pltpu.CompilerParams(dimension_semantics=("arbitrary",),
                                             vmem_limit_bytes=VMEM_LIMIT),
        name="ada",
    )(c_all, w_ada, b_ada)


def _modulated(x_ref, sc_ref, sh_ref, g_ref):
    u = _rms(x_ref[...]) * g_ref[...] * (1.0 + sc_ref[...]) + sh_ref[...]
    sb, tb, d = u.shape
    return u.reshape(sb * tb, d).astype(BF16)


def _rope_pair(chunk, cs):
    t = chunk * cs
    return t + pltpu.roll(t, LANES // 2, axis=2)


def _latent_and_query(x_ref, sc_ref, sh_ref, gattn_ref, wa_ref, gq_ref, wqn_ref, gkv_ref, cs_ref,
                      ckv_ref, kr_ref):
    sb, tb, _ = x_ref.shape
    u = _modulated(x_ref, sc_ref, sh_ref, gattn_ref)
    pa = _dot(u, wa_ref[...])
    ckv = _rms(pa[:, Q_LORA:Q_LORA + KV_LORA]) * gkv_ref[...]
    ckv_ref[...] = ckv.reshape(sb, tb, KV_LORA)
    kr2 = _rope_pair(pa[:, Q_LORA + KV_LORA:].reshape(sb, tb, LANES), cs_ref[...])
    kr_ref[...] = kr2[:, :, :ROPE_DIM]
    qn = (_rms(pa[:, :Q_LORA]) * gq_ref[...]).astype(BF16)
    q_nope = _dot(qn, wqn_ref[...])
    return ckv, kr2, qn, q_nope


def _pre_sample_kernel(x_ref, sc_ref, sh_ref, gattn_ref, wa_ref, gq_ref, wqn_ref, gkv_ref, cs_ref,
                       wqp_ref, wuk_ref, qcat_ref, ckv_ref, kr_ref):
    sb, tb, _ = x_ref.shape
    _, _, qn, q_nope = _latent_and_query(x_ref, sc_ref, sh_ref, gattn_ref, wa_ref, gq_ref, wqn_ref,
                                         gkv_ref, cs_ref, ckv_ref, kr_ref)
    q_pair = _dot(qn, wqp_ref[...])
    for h in range(N_HEADS):
        q_lat = _dot(q_nope[:, h * NOPE_DIM:(h + 1) * NOPE_DIM].astype(BF16), wuk_ref[h]) * ATTN_SCALE
        q_rope = _rope_pair(q_pair[:, h * LANES:(h + 1) * LANES].reshape(sb, tb, LANES), cs_ref[...])
        qcat_ref[h, :, :, 0:KV_LORA] = q_lat.reshape(sb, tb, KV_LORA)
        qcat_ref[h, :, :, KV_LORA:QK_DIM] = q_rope * ATTN_SCALE


def _pre_prompt_kernel(x_ref, sc_ref, sh_ref, gattn_ref, wa_ref, gq_ref, wqn_ref, gkv_ref, cs_ref,
                       wqpT_ref, wukT_ref, csT_ref, qT_ref, ckv_ref, kr_ref, kcat_ref, kT_ref, *, tq):
    _, tb, _ = x_ref.shape
    ckv, kr2, qn, q_nope = _latent_and_query(x_ref, sc_ref, sh_ref, gattn_ref, wa_ref, gq_ref, wqn_ref,
                                             gkv_ref, cs_ref, ckv_ref, kr_ref)
    lane = lax.broadcasted_iota(jnp.int32, kr2.shape, 2)
    kcat_ref[:, :, 0:KV_LORA] = ckv.reshape(1, tb, KV_LORA).astype(BF16)
    kcat_ref[:, :, KV_LORA:QK_DIM] = jnp.where(lane < ROPE_DIM, kr2, 0.0).astype(BF16)
    ckvT = ckv.T.astype(BF16)
    csT = csT_ref[...]
    scale = ATTN_SCALE * LOG2_E
    for a in range(tb // tq):
        kT_ref[0, a] = ckvT[:, a * tq:(a + 1) * tq]
    for h in range(N_HEADS):
        q_latT = _dot_nt(wukT_ref[h], q_nope[:, h * NOPE_DIM:(h + 1) * NOPE_DIM].astype(BF16))
        t = _dot_nt(wqpT_ref[h], qn) * csT
        q_ropeT = t + jnp.concatenate([t[LANES // 2:], t[:LANES // 2]], axis=0)
        for a in range(tb // tq):
            cols = slice(a * tq, (a + 1) * tq)
            qT_ref[0, a, 0:KV_LORA, h * tq:(h + 1) * tq] = (q_latT[:, cols] * scale).astype(BF16)
            qT_ref[0, a, KV_LORA:QK_DIM, h * tq:(h + 1) * tq] = (q_ropeT[:, cols] * scale).astype(BF16)


def _pre_common_specs(sb, tb, w_a, w_qn):
    tile = lambda w: pl.BlockSpec((sb, tb, w), lambda i, j: (i, j, 0))
    mod_spec = lambda chunk: pl.BlockSpec((sb, 1, D_MODEL), lambda i, j: (i, 0, chunk))
    in_specs = [tile(D_MODEL), mod_spec(1), mod_spec(0),
                _const_spec((1, D_MODEL)), _const_spec(w_a.shape), _const_spec((1, Q_LORA)),
                _const_spec(w_qn.shape), _const_spec((1, KV_LORA)),
                pl.BlockSpec((tb, LANES), lambda i, j: (j, 0))]
    return tile, in_specs


def _pre_sample(x, mod, g_attn, w_a, g_q, w_qn, g_kv, cs, w_qp, w_uk, *, sb):
    s_tot, tb, _ = x.shape
    tile, in_specs = _pre_common_specs(sb, tb, w_a, w_qn)
    return pl.pallas_call(
        _pre_sample_kernel,
        out_shape=[jax.ShapeDtypeStruct((N_HEADS, s_tot, tb, QK_DIM), F32),
                   jax.ShapeDtypeStruct((s_tot, tb, KV_LORA), F32),
                   jax.ShapeDtypeStruct((s_tot, tb, ROPE_DIM), F32)],
        grid=(s_tot // sb, 1),
        in_specs=in_specs + [_const_spec(w_qp.shape), _const_spec(w_uk.shape)],
        out_specs=[pl.BlockSpec((N_HEADS, sb, tb, QK_DIM), lambda i, j: (0, i, j, 0)),
                   tile(KV_LORA), tile(ROPE_DIM)],
        compiler_params=pltpu.CompilerParams(dimension_semantics=("arbitrary", "arbitrary"),
                                             vmem_limit_bytes=VMEM_LIMIT),
        name="pre_sample",
    )(x, mod, mod, g_attn, w_a, g_q, w_qn, g_kv, cs, w_qp, w_uk)


def _pre_prompt(x, mod, g_attn, w_a, g_q, w_qn, g_kv, cs, w_qpT, w_ukT, csT, *, tb, tq):
    b, t_tot, _ = x.shape
    per = tb // tq
    tile, in_specs = _pre_common_specs(1, tb, w_a, w_qn)
    return pl.pallas_call(
        functools.partial(_pre_prompt_kernel, tq=tq),
        out_shape=[jax.ShapeDtypeStruct((b, t_tot // tq, QK_DIM, N_HEADS * tq), BF16),
                   jax.ShapeDtypeStruct((b, t_tot, KV_LORA), F32),
                   jax.ShapeDtypeStruct((b, t_tot, ROPE_DIM), F32),
                   jax.ShapeDtypeStruct((b, t_tot, QK_DIM), BF16),
                   jax.ShapeDtypeStruct((b, t_tot // tq, KV_LORA, tq), BF16)],
        grid=(b, t_tot // tb),
        in_specs=in_specs + [_const_spec(w_qpT.shape), _const_spec(w_ukT.shape),
                             pl.BlockSpec((LANES, tb), lambda i, j: (0, j))],
        out_specs=[pl.BlockSpec((1, per, QK_DIM, N_HEADS * tq), lambda i, j: (i, j, 0, 0)),
                   tile(KV_LORA), tile(ROPE_DIM), tile(QK_DIM),
                   pl.BlockSpec((1, per, KV_LORA, tq), lambda i, j: (i, j, 0, 0))],
        compiler_params=pltpu.CompilerParams(dimension_semantics=("arbitrary", "arbitrary"),
                                             vmem_limit_bytes=VMEM_LIMIT),
        name="pre_prompt",
    )(x, mod, mod, g_attn, w_a, g_q, w_qn, g_kv, cs, w_qpT, w_ukT, csT)


HEAD_LOOKAHEAD = 3


def _attn_prompt_kernel(qT_ref, k_ref, kT_ref, wuvT_ref, o_ref, m_ref, l_ref, acc_ref, ahead_ref, *,
                        tq, lookahead):
    i = pl.program_id(1)
    m_ref[...] = jnp.full(m_ref.shape, NEG_BIG, F32)
    l_ref[...] = jnp.zeros(l_ref.shape, F32)
    acc_ref[...] = jnp.zeros(acc_ref.shape, F32)
    heads = [slice(h * tq, (h + 1) * tq) for h in range(N_HEADS)]

    def scores(j, h):
        keys = k_ref[0, pl.ds(pl.multiple_of(j * tq, tq), tq), :]
        return _dot(keys, qT_ref[0, 0, :, heads[h]])

    def chunk(j, visible, has_next):
        k_latT = kT_ref[0, j]
        pending = [ahead_ref[h] for h in range(lookahead)]
        for h in range(N_HEADS):
            if h + lookahead < N_HEADS:
                pending.append(scores(j, h + lookahead))
            elif has_next:
                ahead_ref[h + lookahead - N_HEADS] = scores(j + 1, h + lookahead - N_HEADS)
            sT = pending.pop(0)
            if visible is not None:
                sT = jnp.where(visible, sT, NEG_BIG)
            m_prev = m_ref[:, heads[h]]
            m_new = jnp.maximum(m_prev, jnp.max(sT, axis=0, keepdims=True))
            alpha = jnp.exp2(m_prev - m_new)
            pT = jnp.exp2(sT - m_new)
            l_ref[:, heads[h]] = alpha * l_ref[:, heads[h]] + jnp.sum(pT, axis=0, keepdims=True)
            acc_ref[:, heads[h]] = alpha * acc_ref[:, heads[h]] + _dot(k_latT, pT.astype(BF16))
            m_ref[:, heads[h]] = m_new

    for h in range(lookahead):
        ahead_ref[h] = scores(0, h)

    def full_chunk_pair(jj, carry):
        chunk(2 * jj, None, True)
        chunk(2 * jj + 1, None, True)
        return carry

    lax.fori_loop(0, i // 2, full_chunk_pair, 0)

    @pl.when(i % 2 == 1)
    def _():
        chunk(i - 1, None, True)

    k_tok = lax.broadcasted_iota(jnp.int32, (tq, tq), 0)
    q_tok = lax.broadcasted_iota(jnp.int32, (tq, tq), 1)
    chunk(i, k_tok <= q_tok, False)

    for h in range(N_HEADS):
        oT = (acc_ref[:, heads[h]] / l_ref[:, heads[h]]).astype(BF16)
        o_ref[0, :, h * V_DIM:(h + 1) * V_DIM] = _dot(wuvT_ref[h], oT).T


def _attn_prompt(qT, kcat, kT, w_uvT, *, tq):
    b, n_q, _, rows = qT.shape
    assert tq & (tq - 1) == 0 and rows == N_HEADS * tq
    t = n_q * tq
    return pl.pallas_call(
        functools.partial(_attn_prompt_kernel, tq=tq, lookahead=HEAD_LOOKAHEAD),
        out_shape=jax.ShapeDtypeStruct((b, t, N_HEADS * V_DIM), F32),
        grid=(b, n_q),
        in_specs=[pl.BlockSpec((1, 1, QK_DIM, rows), lambda bi, i: (bi, i, 0, 0)),
                  pl.BlockSpec((1, t, QK_DIM), lambda bi, i: (bi, 0, 0)),
                  pl.BlockSpec((1, n_q, KV_LORA, tq), lambda bi, i: (bi, 0, 0, 0)),
                  _const_spec(w_uvT.shape)],
        out_specs=pl.BlockSpec((1, tq, N_HEADS * V_DIM), lambda bi, i: (bi, i, 0)),
        scratch_shapes=[pltpu.VMEM((1, rows), F32), pltpu.VMEM((1, rows), F32),
                        pltpu.VMEM((KV_LORA, rows), F32), pltpu.VMEM((HEAD_LOOKAHEAD, tq, tq), F32)],
        compiler_params=pltpu.CompilerParams(dimension_semantics=("arbitrary", "arbitrary"),
                                             vmem_limit_bytes=VMEM_LIMIT),
        name="attn_prompt",
    )(qT, kcat, kT, w_uvT)


DMA_UNROLL = 8
SAMPLE_KEY_BLOCKS = 16
SAMPLE_LOOKAHEAD = 3
SEQS_PER_STEP = 2


def _attn_sample_kernel(pt_ref, q_ref, nk_ref, nr_ref, wuv_ref, ckv_hbm, krT_hbm, o_ref,
                        kbuf, rbuf, nkbuf, nrbuf, sem, *, n_pages, dec_t, n_blocks, lookahead):
    g = pl.program_id(0)
    n_steps = pl.num_programs(0)
    n_seq = n_steps * SEQS_PER_STEP
    rows = N_HEADS * dec_t
    past = n_pages * PAGE_SIZE

    def page_copies(page, p, slot):
        pos = pl.ds(pl.multiple_of(p * PAGE_SIZE, PAGE_SIZE), PAGE_SIZE)
        return (pltpu.make_async_copy(ckv_hbm.at[page], kbuf.at[slot, pos], sem.at[0, slot]),
                pltpu.make_async_copy(krT_hbm.at[page], rbuf.at[slot, :, pos], sem.at[1, slot]))

    def start_fetch(seq, slot):
        for p in range(n_pages):
            for cp in page_copies(pt_ref[seq, p], p, slot):
                cp.start()

    def wait_fetch(slot):
        def body(p, carry):
            for cp in page_copies(0, p, slot):
                cp.wait()
            return carry
        lax.fori_loop(0, n_pages, body, 0, unroll=DMA_UNROLL)

    def partial_softmax(s, k_lat):
        m_b = jnp.max(s, axis=-1, keepdims=True)
        p = jnp.exp(s - m_b)
        return m_b, jnp.sum(p, axis=-1, keepdims=True), _dot(p.astype(BF16), k_lat)

    def attend(slot):
        q = q_ref[:, slot].reshape(rows, QK_DIM)
        q_lat = q[:, :KV_LORA].astype(BF16)
        q_rope = q[:, KV_LORA:KV_LORA + ROPE_DIM].astype(BF16)

        def block_scores(b):
            ks = slice(b * past // n_blocks, (b + 1) * past // n_blocks)
            k_lat = kbuf[slot, ks, :].astype(BF16)
            return _dot_nt(q_lat, k_lat) + _dot(q_rope, rbuf[slot, :, ks].astype(BF16)), k_lat

        pending = [block_scores(b) for b in range(lookahead)]
        parts = []
        for b in range(n_blocks):
            if b + lookahead < n_blocks:
                pending.append(block_scores(b + lookahead))
            parts.append(partial_softmax(*pending.pop(0)))

        nkbuf[slot, 0:dec_t, :] = nk_ref[slot]
        nrbuf[slot, 0:dec_t, :] = nr_ref[slot]
        n_lat = nkbuf[slot].astype(BF16)
        s_new = _dot_nt(q_lat, n_lat) + _dot_nt(q_rope, nrbuf[slot].astype(BF16))
        q_tok = lax.broadcasted_iota(jnp.int32, (N_HEADS, dec_t, PAGE_SIZE), 1).reshape(rows, PAGE_SIZE)
        k_tok = lax.broadcasted_iota(jnp.int32, (rows, PAGE_SIZE), 1)
        s_new = jnp.where(k_tok <= q_tok, s_new, NEG_BIG)
        parts.append(partial_softmax(s_new, n_lat))

        m = functools.reduce(jnp.maximum, [m_b for m_b, _, _ in parts])
        weights = [jnp.exp(m_b - m) for m_b, _, _ in parts]
        l = functools.reduce(jnp.add, [w_b * l_b for w_b, (_, l_b, _) in zip(weights, parts)])
        acc = functools.reduce(jnp.add, [w_b * acc_b for w_b, (_, _, acc_b) in zip(weights, parts)])

        o = (acc / l).astype(BF16)
        for h in range(N_HEADS):
            o_ref[slot, :, h * V_DIM:(h + 1) * V_DIM] = _dot(o[h * dec_t:(h + 1) * dec_t], wuv_ref[h])

    first = g * SEQS_PER_STEP

    @pl.when(g == 0)
    def _():
        start_fetch(0, 0)
        nkbuf[...] = jnp.zeros(nkbuf.shape, F32)
        nrbuf[...] = jnp.zeros(nrbuf.shape, F32)

    wait_fetch(0)
    start_fetch(first + 1, 1)
    attend(0)
    wait_fetch(1)
    start_fetch(jnp.minimum(first + 2, n_seq - 1), 0)
    attend(1)

    @pl.when(g == n_steps - 1)
    def _():
        wait_fetch(0)


def _attn_sample(page_table, qcat, new_ckv, new_kr, w_uv, cache_ckv, cache_krT):
    _, n_seq, dec_t, _ = qcat.shape
    n_pages = page_table.shape[1]
    past = n_pages * PAGE_SIZE
    per = SEQS_PER_STEP
    grid_spec = pltpu.PrefetchScalarGridSpec(
        num_scalar_prefetch=1,
        grid=(n_seq // per,),
        in_specs=[pl.BlockSpec((N_HEADS, per, dec_t, QK_DIM), lambda g, pt: (0, g, 0, 0)),
                  pl.BlockSpec((per, dec_t, KV_LORA), lambda g, pt: (g, 0, 0)),
                  pl.BlockSpec((per, dec_t, ROPE_DIM), lambda g, pt: (g, 0, 0)),
                  pl.BlockSpec(w_uv.shape, lambda g, pt: (0, 0, 0), pipeline_mode=pl.Buffered(1)),
                  pl.BlockSpec(memory_space=pl.ANY),
                  pl.BlockSpec(memory_space=pl.ANY)],
        out_specs=pl.BlockSpec((per, dec_t, N_HEADS * V_DIM), lambda g, pt: (g, 0, 0)),
        scratch_shapes=[pltpu.VMEM((per, past, KV_LORA), F32),
                        pltpu.VMEM((per, ROPE_DIM, past), F32),
                        pltpu.VMEM((per, PAGE_SIZE, KV_LORA), F32),
                        pltpu.VMEM((per, PAGE_SIZE, ROPE_DIM), F32),
                        pltpu.SemaphoreType.DMA((2, per))],
    )
    return pl.pallas_call(
        functools.partial(_attn_sample_kernel, n_pages=n_pages, dec_t=dec_t, n_blocks=SAMPLE_KEY_BLOCKS,
                          lookahead=SAMPLE_LOOKAHEAD),
        out_shape=jax.ShapeDtypeStruct((n_seq, dec_t, N_HEADS * V_DIM), F32),
        grid_spec=grid_spec,
        compiler_params=pltpu.CompilerParams(dimension_semantics=("arbitrary",),
                                             vmem_limit_bytes=VMEM_LIMIT),
        name="attn_sample",
    )(page_table, qcat, new_ckv, new_kr, w_uv, cache_ckv, cache_krT)


def _post1_kernel(x_ref, attn_ref, sc_ref, sh_ref, g1_ref, gattn_ref, wb_ref, wc_ref, cw_ref, wo_ref,
                  prev_ref, x1_ref, nconv_ref, carry_ref, *, sb, tb):
    m = sb * tb

    @pl.when(pl.program_id(1) == 0)
    def _():
        carry_ref[...] = prev_ref[...]

    u = _modulated(x_ref, sc_ref, sh_ref, gattn_ref)
    pb = _dot(u, wb_ref[...])
    v = (pb[:, D_CONV:2 * D_CONV] * pb[:, 2 * D_CONV:]).reshape(sb, tb, D_CONV)
    p0 = carry_ref[:, 0:1, :]
    p1 = carry_ref[:, 1:2, :]
    t = lax.broadcasted_iota(jnp.int32, v.shape, 1)
    r1 = pltpu.roll(v, 1, axis=1)
    r2 = pltpu.roll(v, 2, axis=1)
    v1 = jnp.where(t >= 1, r1, p1)
    v2 = jnp.where(t >= 2, r2, jnp.where(t == 1, p1, p0))
    cw = cw_ref[...]
    z = cw[0:1, :] * v2 + cw[1:2, :] * v1 + cw[2:3, :] * v
    conv_out = pb[:, :D_CONV] * z.reshape(m, D_CONV)
    tail = r2[:, 0:CONV_W - 1, :]
    carry_ref[...] = tail
    nconv_ref[...] = tail

    g = jax.nn.sigmoid(_dot(u, wc_ref[...]))
    merged = g[:, :D_MODEL] * attn_ref[...].reshape(m, D_MODEL) + g[:, D_MODEL:] * conv_out
    proj = _dot(merged.astype(BF16), wo_ref[...]).reshape(sb, tb, D_MODEL)
    x1_ref[...] = x_ref[...] + g1_ref[...] * proj


def _post1(x, attn, mod, g_attn, w_b, w_c, conv_w, w_o, prev, *, sb, tb):
    s_tot, t_tot, _ = x.shape
    tile = pl.BlockSpec((sb, tb, D_MODEL), lambda i, j: (i, j, 0))
    mod_spec = lambda chunk: pl.BlockSpec((sb, 1, D_MODEL), lambda i, j: (i, 0, chunk))
    state = pl.BlockSpec((sb, CONV_W - 1, D_CONV), lambda i, j: (i, 0, 0))
    return pl.pallas_call(
        functools.partial(_post1_kernel, sb=sb, tb=tb),
        out_shape=[jax.ShapeDtypeStruct(x.shape, F32),
                   jax.ShapeDtypeStruct((s_tot, CONV_W - 1, D_CONV), F32)],
        grid=(s_tot // sb, t_tot // tb),
        in_specs=[tile, tile, mod_spec(1), mod_spec(0), mod_spec(2),
                  _const_spec((1, D_MODEL)), _const_spec(w_b.shape), _const_spec(w_c.shape),
                  _const_spec(conv_w.shape), _const_spec(w_o.shape), state],
        out_specs=[tile, state],
        scratch_shapes=[pltpu.VMEM((sb, CONV_W - 1, D_CONV), F32)],
        compiler_params=pltpu.CompilerParams(dimension_semantics=("arbitrary", "arbitrary"),
                                             vmem_limit_bytes=VMEM_LIMIT),
        name="post1",
    )(x, attn, mod, mod, mod, g_attn, w_b, w_c, conv_w, w_o, prev)


def _post2_kernel(x1_ref, sc_ref, sh_ref, g2_ref, gmlp_ref, w1_ref, w2_ref, *rest, sb, tb, ff_chunk):
    y_ref = rest[-1]
    m = sb * tb
    u = _modulated(x1_ref, sc_ref, sh_ref, gmlp_ref)
    acc = jnp.zeros((m, D_MODEL), F32)
    for c in range(D_FF // ff_chunk):
        cols = slice(c * ff_chunk, (c + 1) * ff_chunk)
        hdn = jnp.square(jnp.maximum(_dot(u, w1_ref[:, cols]), 0.0))
        acc = acc + _dot(hdn.astype(BF16), w2_ref[cols, :])
    x2 = x1_ref[...] + g2_ref[...] * acc.reshape(sb, tb, D_MODEL)
    y_ref[...] = _rms(x2) * rest[0][...] if len(rest) == 2 else x2


def _post2(x1, mod, g_mlp, w_1, w_2, maybe_g_final, *, sb, tb, ff_chunk):
    s_tot, t_tot, _ = x1.shape
    tile = pl.BlockSpec((sb, tb, D_MODEL), lambda i, j: (i, j, 0))
    mod_spec = lambda chunk: pl.BlockSpec((sb, 1, D_MODEL), lambda i, j: (i, 0, chunk))
    return pl.pallas_call(
        functools.partial(_post2_kernel, sb=sb, tb=tb, ff_chunk=ff_chunk),
        out_shape=jax.ShapeDtypeStruct(x1.shape, F32),
        grid=(s_tot // sb, t_tot // tb),
        in_specs=[tile, mod_spec(4), mod_spec(3), mod_spec(5),
                  _const_spec((1, D_MODEL)), _const_spec(w_1.shape), _const_spec(w_2.shape)]
                 + [_const_spec((1, D_MODEL)) for _ in maybe_g_final],
        out_specs=tile,
        compiler_params=pltpu.CompilerParams(dimension_semantics=("arbitrary", "arbitrary"),
                                             vmem_limit_bytes=VMEM_LIMIT),
        name="post2",
    )(x1, mod, mod, mod, g_mlp, w_1, w_2, *maybe_g_final)


def _rope_table(pos):
    inv = ROPE_THETA ** (-jnp.arange(0, ROPE_DIM, 2, dtype=F32) / ROPE_DIM)
    ang = pos.astype(F32)[:, None] * inv[None, :]
    cos, sin = jnp.cos(ang), jnp.sin(ang)
    return jnp.concatenate([cos, cos, -sin, sin], axis=-1)


def _swap_halves(w):
    half = w.shape[-1] // 2
    return jnp.concatenate([w[..., half:], w[..., :half]], axis=-1)


def _layer_weights(w_in, w_q_b, w_kv_b, w_o, w_1, w_2):
    s1, s2 = Q_LORA + KV_LORA, Q_LORA + KV_LORA + ROPE_DIM
    s3 = s2 + 3 * D_CONV
    w_kr = w_in[:, s1:s2]
    wq = w_q_b.reshape(Q_LORA, N_HEADS, NOPE_DIM + ROPE_DIM)
    wq_rope = wq[:, :, NOPE_DIM:]
    wq_pair = jnp.concatenate([wq_rope, _swap_halves(wq_rope)], axis=-1)
    wkv = w_kv_b.reshape(KV_LORA, N_HEADS, NOPE_DIM + V_DIM)
    w_uk, w_uv = wkv[:, :, :NOPE_DIM], wkv[:, :, NOPE_DIM:]
    weights = dict(
        w_a=jnp.concatenate([w_in[:, :s1], w_kr, _swap_halves(w_kr)], axis=1),
        w_b=w_in[:, s2:s3],
        w_c=w_in[:, s3:],
        w_qn=wq[:, :, :NOPE_DIM].reshape(Q_LORA, N_HEADS * NOPE_DIM),
        w_qp=wq_pair.reshape(Q_LORA, N_HEADS * LANES),
        w_qpT=jnp.transpose(wq_pair, (1, 2, 0)),
        w_uk=jnp.transpose(w_uk, (1, 2, 0)),
        w_ukT=jnp.transpose(w_uk, (1, 0, 2)),
        w_uv=jnp.transpose(w_uv, (1, 0, 2)),
        w_uvT=jnp.transpose(w_uv, (1, 2, 0)),
        w_o=w_o, w_1=w_1, w_2=w_2)
    return {k: v.astype(BF16) for k, v in weights.items()}


PROMPT_TILE = 512
ATTN_TILE = 256
SAMPLE_SEQS = 64
FF_CHUNK = 1024


def kernel(x_prompt, x_sample, cache_ckv, cache_krope, state_conv, page_table, c_prompt, c_sample,
           w_ada, b_ada, g_attn, w_in, g_q, w_q_b, g_kv, w_kv_b, conv_w, w_o, g_mlp, w_1, w_2, g_final):
    depth = w_in.shape[0]
    batch, seq, _ = x_prompt.shape
    dec_b, dec_t, _ = x_sample.shape
    past_len = page_table.shape[1] * PAGE_SIZE
    cs_p = _rope_table(jnp.arange(seq, dtype=jnp.int32))
    cs_s = _rope_table(past_len + jnp.arange(dec_t, dtype=jnp.int32))
    n_c = batch + dec_b
    c_all = jnp.concatenate([c_prompt, c_sample, jnp.zeros((-n_c % 8, D_MODEL), F32)], axis=0)
    row = lambda g: g.reshape(1, -1)

    hp, hs = x_prompt, x_sample
    outs = [[] for _ in range(6)]
    for l in range(depth):
        w = _layer_weights(w_in[l], w_q_b[l], w_kv_b[l], w_o[l], w_1[l], w_2[l])
        mod = _ada(c_all, w_ada[l], row(b_ada[l]))
        mod_p = mod[:batch].reshape(batch, 1, -1)
        mod_s = mod[batch:n_c].reshape(dec_b, 1, -1)
        shared = (row(g_attn[l]), w["w_a"], row(g_q[l]), w["w_qn"], row(g_kv[l]))

        qT, ckv_p, kr_p, kcat, kT = _pre_prompt(hp, mod_p, *shared, cs_p, w["w_qpT"], w["w_ukT"], cs_p.T,
                                                tb=PROMPT_TILE, tq=ATTN_TILE)
        attn_p = _attn_prompt(qT, kcat, kT, w["w_uvT"], tq=ATTN_TILE)
        buf0 = jnp.zeros((batch, CONV_W - 1, D_CONV), F32)
        x1_p, conv_p = _post1(hp, attn_p, mod_p, row(g_attn[l]), w["w_b"], w["w_c"], conv_w[l], w["w_o"], buf0,
                              sb=1, tb=PROMPT_TILE)

        qcat_s, ckv_s, kr_s = _pre_sample(hs, mod_s, *shared, cs_s, w["w_qp"], w["w_uk"], sb=SAMPLE_SEQS)
        attn_s = _attn_sample(page_table, qcat_s, ckv_s, kr_s, w["w_uv"], cache_ckv[l],
                              jnp.swapaxes(cache_krope[l], 1, 2))
        x1_s, conv_s = _post1(hs, attn_s, mod_s, row(g_attn[l]), w["w_b"], w["w_c"], conv_w[l], w["w_o"],
                              state_conv[l], sb=SAMPLE_SEQS, tb=dec_t)

        g_fin = (row(g_final),) if l == depth - 1 else ()
        hp = _post2(x1_p, mod_p, row(g_mlp[l]), w["w_1"], w["w_2"], g_fin, sb=1, tb=PROMPT_TILE,
                    ff_chunk=FF_CHUNK)
        hs = _post2(x1_s, mod_s, row(g_mlp[l]), w["w_1"], w["w_2"], g_fin, sb=SAMPLE_SEQS, tb=dec_t,
                    ff_chunk=FF_CHUNK)
        for acc, val in zip(outs, (ckv_p, kr_p, conv_p, ckv_s, kr_s, conv_s)):
            acc.append(val)
    return (hp, hs) + tuple(jnp.stack(o) for o in outs)
```

```python
import functools

import jax
import jax.numpy as jnp
from jax import lax
from jax.experimental import pallas as pl
from jax.experimental.pallas import tpu as pltpu

D_MODEL = 1024
N_HEADS = 8
Q_LORA = 384
KV_LORA = 256
NOPE_DIM = 128
ROPE_DIM = 64
V_DIM = D_MODEL // N_HEADS
D_CONV = D_MODEL
CONV_W = 3
D_FF = 4 * D_MODEL
PAGE_SIZE = 128
ROPE_THETA = 10000.0
EPS = 1e-6
ATTN_SCALE = (NOPE_DIM + ROPE_DIM) ** -0.5
LOG2_E = 1.4426950408889634

LANES = 128
QK_DIM = KV_LORA + LANES
NEG_BIG = -1e30
VMEM_LIMIT = 56 * 1024 * 1024

F32 = jnp.float32
BF16 = jnp.bfloat16


def _const_spec(shape):
    zeros = (0,) * len(shape)
    return pl.BlockSpec(shape, lambda *_: zeros, pipeline_mode=pl.Buffered(1))


def _dot(a, b):
    return jnp.dot(a, b, preferred_element_type=F32)


def _dot_nt(a, b):
    return lax.dot_general(a, b, (((1,), (1,)), ((), ())), preferred_element_type=F32)


def _rms(x):
    return x * lax.rsqrt(jnp.mean(x * x, axis=-1, keepdims=True) + EPS)


def _ada_kernel(c_ref, w_ref, b_ref, o_ref):
    c = c_ref[...]
    a = (c * jax.nn.sigmoid(c)).astype(BF16)
    o_ref[...] = _dot(a, w_ref[...].astype(BF16)) + b_ref[...]


def _ada(c_all, w_ada, b_ada):
    rows = c_all.shape[0]
    n_out = w_ada.shape[1]
    bn = D_MODEL
    return pl.pallas_call(
        _ada_kernel,
        out_shape=jax.ShapeDtypeStruct((rows, n_out), F32),
        grid=(n_out // bn,),
        in_specs=[pl.BlockSpec((rows, D_MODEL), lambda j: (0, 0)),
                  pl.BlockSpec((D_MODEL, bn), lambda j: (0, j)),
                  pl.BlockSpec((1, bn), lambda j: (0, j))],
        out_specs=pl.BlockSpec((rows, bn), lambda j: (0, j)),
        compiler_params=pltpu.CompilerParams(dimension_semantics=("arbitrary",),
                                             vmem_limit_bytes=VMEM_LIMIT),
        name="ada",
    )(c_all, w_ada, b_ada)


def _modulated(x_ref, sc_ref, sh_ref, g_ref):
    u = _rms(x_ref[...]) * g_ref[...] * (1.0 + sc_ref[...]) + sh_ref[...]
    sb, tb, d = u.shape
    return u.reshape(sb * tb, d).astype(BF16)


def _rope_pair(chunk, cs):
    t = chunk * cs
    return t + pltpu.roll(t, LANES // 2, axis=2)


def _latent_and_query(x_ref, sc_ref, sh_ref, gattn_ref, wa_ref, gq_ref, wqn_ref, gkv_ref, cs_ref, ckv_ref):
    sb, tb, _ = x_ref.shape
    u = _modulated(x_ref, sc_ref, sh_ref, gattn_ref)
    pa = _dot(u, wa_ref[...])
    ckv = _rms(pa[:, Q_LORA:Q_LORA + KV_LORA]) * gkv_ref[...]
    ckv_ref[...] = ckv.reshape(sb, tb, KV_LORA)
    kr2 = _rope_pair(pa[:, Q_LORA + KV_LORA:].reshape(sb, tb, LANES), cs_ref[...])
    qn = (_rms(pa[:, :Q_LORA]) * gq_ref[...]).astype(BF16)
    q_nope = _dot(qn, wqn_ref[...])
    return ckv, kr2, qn, q_nope


def _pre_sample_kernel(x_ref, sc_ref, sh_ref, gattn_ref, wa_ref, gq_ref, wqn_ref, gkv_ref, cs_ref,
                       wqp_ref, wuk_ref, qcat_ref, ckv_ref, kr_ref):
    sb, tb, _ = x_ref.shape
    _, kr2, qn, q_nope = _latent_and_query(x_ref, sc_ref, sh_ref, gattn_ref, wa_ref, gq_ref, wqn_ref,
                                           gkv_ref, cs_ref, ckv_ref)
    kr_ref[...] = kr2[:, :, :ROPE_DIM]
    q_pair = _dot(qn, wqp_ref[...])
    for h in range(N_HEADS):
        q_lat = _dot(q_nope[:, h * NOPE_DIM:(h + 1) * NOPE_DIM].astype(BF16), wuk_ref[h]) * ATTN_SCALE
        q_rope = _rope_pair(q_pair[:, h * LANES:(h + 1) * LANES].reshape(sb, tb, LANES), cs_ref[...])
        qcat_ref[h, :, :, 0:KV_LORA] = q_lat.reshape(sb, tb, KV_LORA)
        qcat_ref[h, :, :, KV_LORA:QK_DIM] = q_rope * ATTN_SCALE


def _pre_prompt_kernel(x_ref, sc_ref, sh_ref, gattn_ref, wa_ref, gq_ref, wqn_ref, gkv_ref, cs_ref,
                       wqpT_ref, wukT_ref, csT_ref, qT_ref, ckv_ref, krT_ref, kcat_ref, kT_ref, *, tq):
    _, tb, _ = x_ref.shape
    ckv, kr2, qn, q_nope = _latent_and_query(x_ref, sc_ref, sh_ref, gattn_ref, wa_ref, gq_ref, wqn_ref,
                                             gkv_ref, cs_ref, ckv_ref)
    krT_ref[0] = kr2.reshape(tb, LANES).T[:ROPE_DIM]
    lane = lax.broadcasted_iota(jnp.int32, kr2.shape, 2)
    kcat_ref[:, :, 0:KV_LORA] = ckv.reshape(1, tb, KV_LORA).astype(BF16)
    kcat_ref[:, :, KV_LORA:QK_DIM] = jnp.where(lane < ROPE_DIM, kr2, 0.0).astype(BF16)
    ckvT = ckv.T.astype(BF16)
    csT = csT_ref[...]
    scale = ATTN_SCALE * LOG2_E
    for a in range(tb // tq):
        kT_ref[0, a] = ckvT[:, a * tq:(a + 1) * tq]
    for h in range(N_HEADS):
        q_latT = _dot_nt(wukT_ref[h], q_nope[:, h * NOPE_DIM:(h + 1) * NOPE_DIM].astype(BF16))
        t = _dot_nt(wqpT_ref[h], qn) * csT
        q_ropeT = t + jnp.concatenate([t[LANES // 2:], t[:LANES // 2]], axis=0)
        for a in range(tb // tq):
            cols = slice(a * tq, (a + 1) * tq)
            qT_ref[0, a, 0:KV_LORA, h * tq:(h + 1) * tq] = (q_latT[:, cols] * scale).astype(BF16)
            qT_ref[0, a, KV_LORA:QK_DIM, h * tq:(h + 1) * tq] = (q_ropeT[:, cols] * scale).astype(BF16)


def _pre_common_specs(sb, tb, w_a, w_qn):
    tile = lambda w: pl.BlockSpec((sb, tb, w), lambda i, j: (i, j, 0))
    mod_spec = lambda chunk: pl.BlockSpec((sb, 1, D_MODEL), lambda i, j: (i, 0, chunk))
    in_specs = [tile(D_MODEL), mod_spec(1), mod_spec(0),
                _const_spec((1, D_MODEL)), _const_spec(w_a.shape), _const_spec((1, Q_LORA)),
                _const_spec(w_qn.shape), _const_spec((1, KV_LORA)),
                pl.BlockSpec((tb, LANES), lambda i, j: (j, 0))]
    return tile, in_specs


def _pre_sample(x, mod, g_attn, w_a, g_q, w_qn, g_kv, cs, w_qp, w_uk, *, sb):
    s_tot, tb, _ = x.shape
    tile, in_specs = _pre_common_specs(sb, tb, w_a, w_qn)
    return pl.pallas_call(
        _pre_sample_kernel,
        out_shape=[jax.ShapeDtypeStruct((N_HEADS, s_tot, tb, QK_DIM), F32),
                   jax.ShapeDtypeStruct((s_tot, tb, KV_LORA), F32),
                   jax.ShapeDtypeStruct((s_tot, tb, ROPE_DIM), F32)],
        grid=(s_tot // sb, 1),
        in_specs=in_specs + [_const_spec(w_qp.shape), _const_spec(w_uk.shape)],
        out_specs=[pl.BlockSpec((N_HEADS, sb, tb, QK_DIM), lambda i, j: (0, i, j, 0)),
                   tile(KV_LORA), tile(ROPE_DIM)],
        compiler_params=pltpu.CompilerParams(dimension_semantics=("arbitrary", "arbitrary"),
                                             vmem_limit_bytes=VMEM_LIMIT),
        name="pre_sample",
    )(x, mod, mod, g_attn, w_a, g_q, w_qn, g_kv, cs, w_qp, w_uk)


def _pre_prompt(x, mod, g_attn, w_a, g_q, w_qn, g_kv, cs, w_qpT, w_ukT, csT, *, tb, tq):
    b, t_tot, _ = x.shape
    per = tb // tq
    tile, in_specs = _pre_common_specs(1, tb, w_a, w_qn)
    return pl.pallas_call(
        functools.partial(_pre_prompt_kernel, tq=tq),
        out_shape=[jax.ShapeDtypeStruct((b, t_tot // tq, QK_DIM, N_HEADS * tq), BF16),
                   jax.ShapeDtypeStruct((b, t_tot, KV_LORA), F32),
                   jax.ShapeDtypeStruct((b, ROPE_DIM, t_tot), F32),
                   jax.ShapeDtypeStruct((b, t_tot, QK_DIM), BF16),
                   jax.ShapeDtypeStruct((b, t_tot // tq, KV_LORA, tq), BF16)],
        grid=(b, t_tot // tb),
        in_specs=in_specs + [_const_spec(w_qpT.shape), _const_spec(w_ukT.shape),
                             pl.BlockSpec((LANES, tb), lambda i, j: (0, j))],
        out_specs=[pl.BlockSpec((1, per, QK_DIM, N_HEADS * tq), lambda i, j: (i, j, 0, 0)),
                   tile(KV_LORA), pl.BlockSpec((1, ROPE_DIM, tb), lambda i, j: (i, 0, j)), tile(QK_DIM),
                   pl.BlockSpec((1, per, KV_LORA, tq), lambda i, j: (i, j, 0, 0))],
        compiler_params=pltpu.CompilerParams(dimension_semantics=("arbitrary", "arbitrary"),
                                             vmem_limit_bytes=VMEM_LIMIT),
        name="pre_prompt",
    )(x, mod, mod, g_attn, w_a, g_q, w_qn, g_kv, cs, w_qpT, w_ukT, csT)


HEAD_LOOKAHEAD = 3


def _attn_prompt_kernel(qT_ref, k_ref, kT_ref, wuvT_ref, o_ref, m_ref, l_ref, acc_ref, ahead_ref, *,
                        tq, lookahead):
    i = pl.program_id(1)
    m_ref[...] = jnp.full(m_ref.shape, NEG_BIG, F32)
    l_ref[...] = jnp.zeros(l_ref.shape, F32)
    acc_ref[...] = jnp.zeros(acc_ref.shape, F32)
    heads = [slice(h * tq, (h + 1) * tq) for h in range(N_HEADS)]

    def scores(j, h):
        keys = k_ref[0, pl.ds(pl.multiple_of(j * tq, tq), tq), :]
        return _dot(keys, qT_ref[0, 0, :, heads[h]])

    def chunk(j, visible, has_next):
        k_latT = kT_ref[0, j]
        pending = [ahead_ref[h] for h in range(lookahead)]
        for h in range(N_HEADS):
            if h + lookahead < N_HEADS:
                pending.append(scores(j, h + lookahead))
            elif has_next:
                ahead_ref[h + lookahead - N_HEADS] = scores(j + 1, h + lookahead - N_HEADS)
            sT = pending.pop(0)
            if visible is not None:
                sT = jnp.where(visible, sT, NEG_BIG)
            m_prev = m_ref[:, heads[h]]
            m_new = jnp.maximum(m_prev, jnp.max(sT, axis=0, keepdims=True))
            alpha = jnp.exp2(m_prev - m_new)
            pT = jnp.exp2(sT - m_new)
            l_ref[:, heads[h]] = alpha * l_ref[:, heads[h]] + jnp.sum(pT, axis=0, keepdims=True)
            acc_ref[:, heads[h]] = alpha * acc_ref[:, heads[h]] + _dot(k_latT, pT.astype(BF16))
            m_ref[:, heads[h]] = m_new

    for h in range(lookahead):
        ahead_ref[h] = scores(0, h)

    def full_chunk_pair(jj, carry):
        chunk(2 * jj, None, True)
        chunk(2 * jj + 1, None, True)
        return carry

    lax.fori_loop(0, i // 2, full_chunk_pair, 0)

    @pl.when(i % 2 == 1)
    def _():
        chunk(i - 1, None, True)

    k_tok = lax.broadcasted_iota(jnp.int32, (tq, tq), 0)
    q_tok = lax.broadcasted_iota(jnp.int32, (tq, tq), 1)
    chunk(i, k_tok <= q_tok, False)

    for h in range(N_HEADS):
        oT = (acc_ref[:, heads[h]] / l_ref[:, heads[h]]).astype(BF16)
        o_ref[0, :, h * V_DIM:(h + 1) * V_DIM] = _dot(wuvT_ref[h], oT).T


def _attn_prompt(qT, kcat, kT, w_uvT, *, tq):
    b, n_q, _, rows = qT.shape
    assert tq & (tq - 1) == 0 and rows == N_HEADS * tq
    t = n_q * tq
    return pl.pallas_call(
        functools.partial(_attn_prompt_kernel, tq=tq, lookahead=HEAD_LOOKAHEAD),
        out_shape=jax.ShapeDtypeStruct((b, t, N_HEADS * V_DIM), F32),
        grid=(b, n_q),
        in_specs=[pl.BlockSpec((1, 1, QK_DIM, rows), lambda bi, i: (bi, i, 0, 0)),
                  pl.BlockSpec((1, t, QK_DIM), lambda bi, i: (bi, 0, 0)),
                  pl.BlockSpec((1, n_q, KV_LORA, tq), lambda bi, i: (bi, 0, 0, 0)),
                  _const_spec(w_uvT.shape)],
        out_specs=pl.BlockSpec((1, tq, N_HEADS * V_DIM), lambda bi, i: (bi, i, 0)),
        scratch_shapes=[pltpu.VMEM((1, rows), F32), pltpu.VMEM((1, rows), F32),
                        pltpu.VMEM((KV_LORA, rows), F32), pltpu.VMEM((HEAD_LOOKAHEAD, tq, tq), F32)],
        compiler_params=pltpu.CompilerParams(dimension_semantics=("arbitrary", "arbitrary"),
                                             vmem_limit_bytes=VMEM_LIMIT),
        name="attn_prompt",
    )(qT, kcat, kT, w_uvT)


DMA_UNROLL = 8
SAMPLE_KEY_BLOCKS = 16
SAMPLE_LOOKAHEAD = 3
SEQS_PER_STEP = 4
FETCH_AHEAD = 2


def _attn_sample_kernel(pt_ref, q_ref, nk_ref, nr_ref, wuv_ref, ckv_hbm, krT_hbm, o_ref,
                        kbuf, rbuf, nkbuf, nrbuf, sem, *, n_pages, dec_t, n_blocks, lookahead):
    g = pl.program_id(0)
    n_steps = pl.num_programs(0)
    n_seq = n_steps * SEQS_PER_STEP
    rows = N_HEADS * dec_t
    past = n_pages * PAGE_SIZE

    def page_copies(page, p, slot):
        pos = pl.ds(pl.multiple_of(p * PAGE_SIZE, PAGE_SIZE), PAGE_SIZE)
        return (pltpu.make_async_copy(ckv_hbm.at[page], kbuf.at[slot, pos], sem.at[0, slot]),
                pltpu.make_async_copy(krT_hbm.at[page], rbuf.at[slot, :, pos], sem.at[1, slot]))

    def start_fetch(seq, slot):
        for p in range(n_pages):
            for cp in page_copies(pt_ref[seq, p], p, slot):
                cp.start()

    def wait_fetch(slot):
        def body(p, carry):
            for cp in page_copies(0, p, slot):
                cp.wait()
            return carry
        lax.fori_loop(0, n_pages, body, 0, unroll=DMA_UNROLL)

    def partial_softmax(s, k_lat):
        m_b = jnp.max(s, axis=-1, keepdims=True)
        p = jnp.exp(s - m_b)
        return m_b, jnp.sum(p, axis=-1, keepdims=True), _dot(p.astype(BF16), k_lat)

    def attend(slot):
        q = q_ref[:, slot].reshape(rows, QK_DIM)
        q_lat = q[:, :KV_LORA].astype(BF16)
        q_rope = q[:, KV_LORA:KV_LORA + ROPE_DIM].astype(BF16)

        def block_scores(b):
            ks = slice(b * past // n_blocks, (b + 1) * past // n_blocks)
            k_lat = kbuf[slot, ks, :].astype(BF16)
            return _dot_nt(q_lat, k_lat) + _dot(q_rope, rbuf[slot, :, ks].astype(BF16)), k_lat

        pending = [block_scores(b) for b in range(lookahead)]
        parts = []
        for b in range(n_blocks):
            if b + lookahead < n_blocks:
                pending.append(block_scores(b + lookahead))
            parts.append(partial_softmax(*pending.pop(0)))

        nkbuf[slot, 0:dec_t, :] = nk_ref[slot]
        nrbuf[slot, 0:dec_t, :] = nr_ref[slot]
        n_lat = nkbuf[slot].astype(BF16)
        s_new = _dot_nt(q_lat, n_lat) + _dot_nt(q_rope, nrbuf[slot].astype(BF16))
        q_tok = lax.broadcasted_iota(jnp.int32, (N_HEADS, dec_t, PAGE_SIZE), 1).reshape(rows, PAGE_SIZE)
        k_tok = lax.broadcasted_iota(jnp.int32, (rows, PAGE_SIZE), 1)
        s_new = jnp.where(k_tok <= q_tok, s_new, NEG_BIG)
        parts.append(partial_softmax(s_new, n_lat))

        m = functools.reduce(jnp.maximum, [m_b for m_b, _, _ in parts])
        weights = [jnp.exp(m_b - m) for m_b, _, _ in parts]
        l = functools.reduce(jnp.add, [w_b * l_b for w_b, (_, l_b, _) in zip(weights, parts)])
        acc = functools.reduce(jnp.add, [w_b * acc_b for w_b, (_, _, acc_b) in zip(weights, parts)])

        o = (acc / l).astype(BF16)
        for h in range(N_HEADS):
            o_ref[slot, :, h * V_DIM:(h + 1) * V_DIM] = _dot(o[h * dec_t:(h + 1) * dec_t], wuv_ref[h])

    first = g * SEQS_PER_STEP

    @pl.when(g == 0)
    def _():
        for a in range(FETCH_AHEAD):
            start_fetch(a, a)
        nkbuf[...] = jnp.zeros(nkbuf.shape, F32)
        nrbuf[...] = jnp.zeros(nrbuf.shape, F32)

    for a in range(SEQS_PER_STEP):
        wait_fetch(a)
        start_fetch(jnp.minimum(first + a + FETCH_AHEAD, n_seq - 1), (a + FETCH_AHEAD) % SEQS_PER_STEP)
        attend(a)

    @pl.when(g == n_steps - 1)
    def _():
        for a in range(FETCH_AHEAD):
            wait_fetch(a)


def _attn_sample(page_table, qcat, new_ckv, new_kr, w_uv, cache_ckv, cache_krT):
    _, n_seq, dec_t, _ = qcat.shape
    n_pages = page_table.shape[1]
    past = n_pages * PAGE_SIZE
    per = SEQS_PER_STEP
    assert n_seq % per == 0 and 0 < FETCH_AHEAD < per
    grid_spec = pltpu.PrefetchScalarGridSpec(
        num_scalar_prefetch=1,
        grid=(n_seq // per,),
        in_specs=[pl.BlockSpec((N_HEADS, per, dec_t, QK_DIM), lambda g, pt: (0, g, 0, 0)),
                  pl.BlockSpec((per, dec_t, KV_LORA), lambda g, pt: (g, 0, 0)),
                  pl.BlockSpec((per, dec_t, ROPE_DIM), lambda g, pt: (g, 0, 0)),
                  pl.BlockSpec(w_uv.shape, lambda g, pt: (0, 0, 0), pipeline_mode=pl.Buffered(1)),
                  pl.BlockSpec(memory_space=pl.ANY),
                  pl.BlockSpec(memory_space=pl.ANY)],
        out_specs=pl.BlockSpec((per, dec_t, N_HEADS * V_DIM), lambda g, pt: (g, 0, 0)),
        scratch_shapes=[pltpu.VMEM((per, past, KV_LORA), F32),
                        pltpu.VMEM((per, ROPE_DIM, past), F32),
                        pltpu.VMEM((per, PAGE_SIZE, KV_LORA), F32),
                        pltpu.VMEM((per, PAGE_SIZE, ROPE_DIM), F32),
                        pltpu.SemaphoreType.DMA((2, per))],
    )
    return pl.pallas_call(
        functools.partial(_attn_sample_kernel, n_pages=n_pages, dec_t=dec_t, n_blocks=SAMPLE_KEY_BLOCKS,
                          lookahead=SAMPLE_LOOKAHEAD),
        out_shape=jax.ShapeDtypeStruct((n_seq, dec_t, N_HEADS * V_DIM), F32),
        grid_spec=grid_spec,
        compiler_params=pltpu.CompilerParams(dimension_semantics=("arbitrary",),
                                             vmem_limit_bytes=VMEM_LIMIT),
        name="attn_sample",
    )(page_table, qcat, new_ckv, new_kr, w_uv, cache_ckv, cache_krT)


def _post1_kernel(x_ref, attn_ref, sc_ref, sh_ref, g1_ref, gattn_ref, wb_ref, wc_ref, cw_ref, wo_ref,
                  prev_ref, x1_ref, nconv_ref, carry_ref, *, sb, tb):
    m = sb * tb

    @pl.when(pl.program_id(1) == 0)
    def _():
        carry_ref[...] = prev_ref[...]

    u = _modulated(x_ref, sc_ref, sh_ref, gattn_ref)
    pb = _dot(u, wb_ref[...])
    v = (pb[:, D_CONV:2 * D_CONV] * pb[:, 2 * D_CONV:]).reshape(sb, tb, D_CONV)
    p0 = carry_ref[:, 0:1, :]
    p1 = carry_ref[:, 1:2, :]
    t = lax.broadcasted_iota(jnp.int32, v.shape, 1)
    r1 = pltpu.roll(v, 1, axis=1)
    r2 = pltpu.roll(v, 2, axis=1)
    v1 = jnp.where(t >= 1, r1, p1)
    v2 = jnp.where(t >= 2, r2, jnp.where(t == 1, p1, p0))
    cw = cw_ref[...]
    z = cw[0:1, :] * v2 + cw[1:2, :] * v1 + cw[2:3, :] * v
    conv_out = pb[:, :D_CONV] * z.reshape(m, D_CONV)
    tail = r2[:, 0:CONV_W - 1, :]
    carry_ref[...] = tail
    nconv_ref[...] = tail

    g = jax.nn.sigmoid(_dot(u, wc_ref[...]))
    merged = g[:, :D_MODEL] * attn_ref[...].reshape(m, D_MODEL) + g[:, D_MODEL:] * conv_out
    proj = _dot(merged.astype(BF16), wo_ref[...]).reshape(sb, tb, D_MODEL)
    x1_ref[...] = x_ref[...] + g1_ref[...] * proj


def _post1(x, attn, mod, g_attn, w_b, w_c, conv_w, w_o, prev, *, sb, tb):
    s_tot, t_tot, _ = x.shape
    tile = pl.BlockSpec((sb, tb, D_MODEL), lambda i, j: (i, j, 0))
    mod_spec = lambda chunk: pl.BlockSpec((sb, 1, D_MODEL), lambda i, j: (i, 0, chunk))
    state = pl.BlockSpec((sb, CONV_W - 1, D_CONV), lambda i, j: (i, 0, 0))
    return pl.pallas_call(
        functools.partial(_post1_kernel, sb=sb, tb=tb),
        out_shape=[jax.ShapeDtypeStruct(x.shape, F32),
                   jax.ShapeDtypeStruct((s_tot, CONV_W - 1, D_CONV), F32)],
        grid=(s_tot // sb, t_tot // tb),
        in_specs=[tile, tile, mod_spec(1), mod_spec(0), mod_spec(2),
                  _const_spec((1, D_MODEL)), _const_spec(w_b.shape), _const_spec(w_c.shape),
                  _const_spec(conv_w.shape), _const_spec(w_o.shape), state],
        out_specs=[tile, state],
        scratch_shapes=[pltpu.VMEM((sb, CONV_W - 1, D_CONV), F32)],
        compiler_params=pltpu.CompilerParams(dimension_semantics=("arbitrary", "arbitrary"),
                                             vmem_limit_bytes=VMEM_LIMIT),
        name="post1",
    )(x, attn, mod, mod, mod, g_attn, w_b, w_c, conv_w, w_o, prev)


def _post2_kernel(x1_ref, sc_ref, sh_ref, g2_ref, gmlp_ref, w1_ref, w2_ref, *rest, sb, tb, ff_chunk):
    y_ref = rest[-1]
    m = sb * tb
    u = _modulated(x1_ref, sc_ref, sh_ref, gmlp_ref)
    acc = jnp.zeros((m, D_MODEL), F32)
    for c in range(D_FF // ff_chunk):
        cols = slice(c * ff_chunk, (c + 1) * ff_chunk)
        hdn = jnp.square(jnp.maximum(_dot(u, w1_ref[:, cols]), 0.0))
        acc = acc + _dot(hdn.astype(BF16), w2_ref[cols, :])
    x2 = x1_ref[...] + g2_ref[...] * acc.reshape(sb, tb, D_MODEL)
    y_ref[...] = _rms(x2) * rest[0][...] if len(rest) == 2 else x2


def _post2(x1, mod, g_mlp, w_1, w_2, maybe_g_final, *, sb, tb, ff_chunk):
    s_tot, t_tot, _ = x1.shape
    tile = pl.BlockSpec((sb, tb, D_MODEL), lambda i, j: (i, j, 0))
    mod_spec = lambda chunk: pl.BlockSpec((sb, 1, D_MODEL), lambda i, j: (i, 0, chunk))
    return pl.pallas_call(
        functools.partial(_post2_kernel, sb=sb, tb=tb, ff_chunk=ff_chunk),
        out_shape=jax.ShapeDtypeStruct(x1.shape, F32),
        grid=(s_tot // sb, t_tot // tb),
        in_specs=[tile, mod_spec(4), mod_spec(3), mod_spec(5),
                  _const_spec((1, D_MODEL)), _const_spec(w_1.shape), _const_spec(w_2.shape)]
                 + [_const_spec((1, D_MODEL)) for _ in maybe_g_final],
        out_specs=tile,
        compiler_params=pltpu.CompilerParams(dimension_semantics=("arbitrary", "arbitrary"),
                                             vmem_limit_bytes=VMEM_LIMIT),
        name="post2",
    )(x1, mod, mod, mod, g_mlp, w_1, w_2, *maybe_g_final)


def _rope_table(pos):
    inv = ROPE_THETA ** (-jnp.arange(0, ROPE_DIM, 2, dtype=F32) / ROPE_DIM)
    ang = pos.astype(F32)[:, None] * inv[None, :]
    cos, sin = jnp.cos(ang), jnp.sin(ang)
    return jnp.concatenate([cos, cos, -sin, sin], axis=-1)


def _swap_halves(w):
    half = w.shape[-1] // 2
    return jnp.concatenate([w[..., half:], w[..., :half]], axis=-1)


def _layer_weights(w_in, w_q_b, w_kv_b, w_o, w_1, w_2):
    s1, s2 = Q_LORA + KV_LORA, Q_LORA + KV_LORA + ROPE_DIM
    s3 = s2 + 3 * D_CONV
    w_kr = w_in[:, s1:s2]
    wq = w_q_b.reshape(Q_LORA, N_HEADS, NOPE_DIM + ROPE_DIM)
    wq_rope = wq[:, :, NOPE_DIM:]
    wq_pair = jnp.concatenate([wq_rope, _swap_halves(wq_rope)], axis=-1)
    wkv = w_kv_b.reshape(KV_LORA, N_HEADS, NOPE_DIM + V_DIM)
    w_uk, w_uv = wkv[:, :, :NOPE_DIM], wkv[:, :, NOPE_DIM:]
    weights = dict(
        w_a=jnp.concatenate([w_in[:, :s1], w_kr, _swap_halves(w_kr)], axis=1),
        w_b=w_in[:, s2:s3],
        w_c=w_in[:, s3:],
        w_qn=wq[:, :, :NOPE_DIM].reshape(Q_LORA, N_HEADS * NOPE_DIM),
        w_qp=wq_pair.reshape(Q_LORA, N_HEADS * LANES),
        w_qpT=jnp.transpose(wq_pair, (1, 2, 0)),
        w_uk=jnp.transpose(w_uk, (1, 2, 0)),
        w_ukT=jnp.transpose(w_uk, (1, 0, 2)),
        w_uv=jnp.transpose(w_uv, (1, 0, 2)),
        w_uvT=jnp.transpose(w_uv, (1, 2, 0)),
        w_o=w_o, w_1=w_1, w_2=w_2)
    return {k: v.astype(BF16) for k, v in weights.items()}


PROMPT_TILE = 512
ATTN_TILE = 256
SAMPLE_SEQS = 64
FF_CHUNK = 1024


def kernel(x_prompt, x_sample, cache_ckv, cache_krope, state_conv, page_table, c_prompt, c_sample,
           w_ada, b_ada, g_attn, w_in, g_q, w_q_b, g_kv, w_kv_b, conv_w, w_o, g_mlp, w_1, w_2, g_final):
    depth = w_in.shape[0]
    batch, seq, _ = x_prompt.shape
    dec_b, dec_t, _ = x_sample.shape
    past_len = page_table.shape[1] * PAGE_SIZE
    cs_p = _rope_table(jnp.arange(seq, dtype=jnp.int32))
    cs_s = _rope_table(past_len + jnp.arange(dec_t, dtype=jnp.int32))
    n_c = batch + dec_b
    c_all = jnp.concatenate([c_prompt, c_sample, jnp.zeros((-n_c % 8, D_MODEL), F32)], axis=0)
    row = lambda g: g.reshape(1, -1)

    hp, hs = x_prompt, x_sample
    outs = [[] for _ in range(6)]
    for l in range(depth):
        w = _layer_weights(w_in[l], w_q_b[l], w_kv_b[l], w_o[l], w_1[l], w_2[l])
        mod = _ada(c_all, w_ada[l], row(b_ada[l]))
        mod_p = mod[:batch].reshape(batch, 1, -1)
        mod_s = mod[batch:n_c].reshape(dec_b, 1, -1)
        shared = (row(g_attn[l]), w["w_a"], row(g_q[l]), w["w_qn"], row(g_kv[l]))

        qT, ckv_p, krT_p, kcat, kT = _pre_prompt(hp, mod_p, *shared, cs_p, w["w_qpT"], w["w_ukT"], cs_p.T,
                                                 tb=PROMPT_TILE, tq=ATTN_TILE)
        kr_p = jnp.swapaxes(krT_p, 1, 2)
        attn_p = _attn_prompt(qT, kcat, kT, w["w_uvT"], tq=ATTN_TILE)
        buf0 = jnp.zeros((batch, CONV_W - 1, D_CONV), F32)
        x1_p, conv_p = _post1(hp, attn_p, mod_p, row(g_attn[l]), w["w_b"], w["w_c"], conv_w[l], w["w_o"], buf0,
                              sb=1, tb=PROMPT_TILE)

        qcat_s, ckv_s, kr_s = _pre_sample(hs, mod_s, *shared, cs_s, w["w_qp"], w["w_uk"], sb=SAMPLE_SEQS)
        attn_s = _attn_sample(page_table, qcat_s, ckv_s, kr_s, w["w_uv"], cache_ckv[l],
                              jnp.swapaxes(cache_krope[l], 1, 2))
        x1_s, conv_s = _post1(hs, attn_s, mod_s, row(g_attn[l]), w["w_b"], w["w_c"], conv_w[l], w["w_o"],
                              state_conv[l], sb=SAMPLE_SEQS, tb=dec_t)

        g_fin = (row(g_final),) if l == depth - 1 else ()
        hp = _post2(x1_p, mod_p, row(g_mlp[l]), w["w_1"], w["w_2"], g_fin, sb=1, tb=PROMPT_TILE,
                    ff_chunk=FF_CHUNK)
        hs = _post2(x1_s, mod_s, row(g_mlp[l]), w["w_1"], w["w_2"], g_fin, sb=SAMPLE_SEQS, tb=dec_t,
                    ff_chunk=FF_CHUNK)
        for acc, val in zip(outs, (ckv_p, kr_p, conv_p, ckv_s, kr_s, conv_s)):
            acc.append(val)
    return (hp, hs) + tuple(jnp.stack(o) for o in outs)
```

```python
import functools

import jax
import jax.numpy as jnp
from jax import lax
from jax.experimental import pallas as pl
from jax.experimental.pallas import tpu as pltpu

D_MODEL = 1024
N_HEADS = 8
Q_LORA = 384
KV_LORA = 256
NOPE_DIM = 128
ROPE_DIM = 64
V_DIM = D_MODEL // N_HEADS
D_CONV = D_MODEL
CONV_W = 3
D_FF = 4 * D_MODEL
PAGE_SIZE = 128
ROPE_THETA = 10000.0
EPS = 1e-6
ATTN_SCALE = (NOPE_DIM + ROPE_DIM) ** -0.5
LOG2_E = 1.4426950408889634

LANES = 128
QK_DIM = KV_LORA + LANES
NEG_BIG = -1e30
VMEM_LIMIT = 56 * 1024 * 1024

F32 = jnp.float32
BF16 = jnp.bfloat16


def _const_spec(shape):
    zeros = (0,) * len(shape)
    return pl.BlockSpec(shape, lambda *_: zeros, pipeline_mode=pl.Buffered(1))


def _dot(a, b):
    return jnp.dot(a, b, preferred_element_type=F32)


def _dot_nt(a, b):
    return lax.dot_general(a, b, (((1,), (1,)), ((), ())), preferred_element_type=F32)


def _rms(x):
    return x * lax.rsqrt(jnp.mean(x * x, axis=-1, keepdims=True) + EPS)


def _ada_kernel(c_ref, w_ref, b_ref, o_ref):
    c = c_ref[...]
    a = (c * jax.nn.sigmoid(c)).astype(BF16)
    o_ref[...] = _dot(a, w_ref[...].astype(BF16)) + b_ref[...]


def _ada(c_all, w_ada, b_ada):
    rows = c_all.shape[0]
    n_out = w_ada.shape[1]
    bn = D_MODEL
    return pl.pallas_call(
        _ada_kernel,
        out_shape=jax.ShapeDtypeStruct((rows, n_out), F32),
        grid=(n_out // bn,),
        in_specs=[pl.BlockSpec((rows, D_MODEL), lambda j: (0, 0)),
                  pl.BlockSpec((D_MODEL, bn), lambda j: (0, j)),
                  pl.BlockSpec((1, bn), lambda j: (0, j))],
        out_specs=pl.BlockSpec((rows, bn), lambda j: (0, j)),
        compiler_params=pltpu.CompilerParams(dimension_semantics=("arbitrary",),
                                             vmem_limit_bytes=VMEM_LIMIT),
        name="ada",
    )(c_all, w_ada, b_ada)


def _modulated(x_ref, sc_ref, sh_ref, g_ref):
    u = _rms(x_ref[...]) * g_ref[...] * (1.0 + sc_ref[...]) + sh_ref[...]
    sb, tb, d = u.shape
    return u.reshape(sb * tb, d).astype(BF16)


def _rope_pair(chunk, cs):
    t = chunk * cs
    return t + pltpu.roll(t, LANES // 2, axis=2)


def _latent_and_query(x_ref, sc_ref, sh_ref, gattn_ref, wa_ref, gq_ref, wqn_ref, gkv_ref, cs_ref, ckv_ref):
    sb, tb, _ = x_ref.shape
    u = _modulated(x_ref, sc_ref, sh_ref, gattn_ref)
    pa = _dot(u, wa_ref[...])
    ckv = _rms(pa[:, Q_LORA:Q_LORA + KV_LORA]) * gkv_ref[...]
    ckv_ref[...] = ckv.reshape(sb, tb, KV_LORA)
    kr2 = _rope_pair(pa[:, Q_LORA + KV_LORA:].reshape(sb, tb, LANES), cs_ref[...])
    qn = (_rms(pa[:, :Q_LORA]) * gq_ref[...]).astype(BF16)
    q_nope = _dot(qn, wqn_ref[...])
    return ckv, kr2, qn, q_nope


def _pre_sample_kernel(x_ref, sc_ref, sh_ref, gattn_ref, wa_ref, gq_ref, wqn_ref, gkv_ref, cs_ref,
                       wqp_ref, wuk_ref, qcat_ref, ckv_ref, kr_ref):
    sb, tb, _ = x_ref.shape
    _, kr2, qn, q_nope = _latent_and_query(x_ref, sc_ref, sh_ref, gattn_ref, wa_ref, gq_ref, wqn_ref,
                                           gkv_ref, cs_ref, ckv_ref)
    kr_ref[...] = kr2[:, :, :ROPE_DIM]
    q_pair = _dot(qn, wqp_ref[...])
    for h in range(N_HEADS):
        q_lat = _dot(q_nope[:, h * NOPE_DIM:(h + 1) * NOPE_DIM].astype(BF16), wuk_ref[h]) * ATTN_SCALE
        q_rope = _rope_pair(q_pair[:, h * LANES:(h + 1) * LANES].reshape(sb, tb, LANES), cs_ref[...])
        qcat_ref[h, :, :, 0:KV_LORA] = q_lat.reshape(sb, tb, KV_LORA)
        qcat_ref[h, :, :, KV_LORA:QK_DIM] = q_rope * ATTN_SCALE


def _pre_prompt_kernel(x_ref, sc_ref, sh_ref, gattn_ref, wa_ref, gq_ref, wqn_ref, gkv_ref, cs_ref,
                       wqpT_ref, wukT_ref, csT_ref, qT_ref, ckv_ref, krT_ref, kcat_ref, kT_ref, *, tq, tk):
    _, tb, _ = x_ref.shape
    ckv, kr2, qn, q_nope = _latent_and_query(x_ref, sc_ref, sh_ref, gattn_ref, wa_ref, gq_ref, wqn_ref,
                                             gkv_ref, cs_ref, ckv_ref)
    krT_ref[0] = kr2.reshape(tb, LANES).T[:ROPE_DIM]
    lane = lax.broadcasted_iota(jnp.int32, kr2.shape, 2)
    kcat_ref[:, :, 0:KV_LORA] = ckv.reshape(1, tb, KV_LORA).astype(BF16)
    kcat_ref[:, :, KV_LORA:QK_DIM] = jnp.where(lane < ROPE_DIM, kr2, 0.0).astype(BF16)
    ckvT = ckv.T.astype(BF16)
    csT = csT_ref[...]
    scale = ATTN_SCALE * LOG2_E
    for a in range(tb // tk):
        kT_ref[0, a] = ckvT[:, a * tk:(a + 1) * tk]
    for h in range(N_HEADS):
        q_latT = _dot_nt(wukT_ref[h], q_nope[:, h * NOPE_DIM:(h + 1) * NOPE_DIM].astype(BF16))
        t = _dot_nt(wqpT_ref[h], qn) * csT
        q_ropeT = t + jnp.concatenate([t[LANES // 2:], t[:LANES // 2]], axis=0)
        for a in range(tb // tq):
            cols = slice(a * tq, (a + 1) * tq)
            qT_ref[0, a, 0:KV_LORA, h * tq:(h + 1) * tq] = (q_latT[:, cols] * scale).astype(BF16)
            qT_ref[0, a, KV_LORA:QK_DIM, h * tq:(h + 1) * tq] = (q_ropeT[:, cols] * scale).astype(BF16)


def _pre_common_specs(sb, tb, w_a, w_qn):
    tile = lambda w: pl.BlockSpec((sb, tb, w), lambda i, j: (i, j, 0))
    mod_spec = lambda chunk: pl.BlockSpec((sb, 1, D_MODEL), lambda i, j: (i, 0, chunk))
    in_specs = [tile(D_MODEL), mod_spec(1), mod_spec(0),
                _const_spec((1, D_MODEL)), _const_spec(w_a.shape), _const_spec((1, Q_LORA)),
                _const_spec(w_qn.shape), _const_spec((1, KV_LORA)),
                pl.BlockSpec((tb, LANES), lambda i, j: (j, 0))]
    return tile, in_specs


def _pre_sample(x, mod, g_attn, w_a, g_q, w_qn, g_kv, cs, w_qp, w_uk, *, sb):
    s_tot, tb, _ = x.shape
    tile, in_specs = _pre_common_specs(sb, tb, w_a, w_qn)
    return pl.pallas_call(
        _pre_sample_kernel,
        out_shape=[jax.ShapeDtypeStruct((N_HEADS, s_tot, tb, QK_DIM), F32),
                   jax.ShapeDtypeStruct((s_tot, tb, KV_LORA), F32),
                   jax.ShapeDtypeStruct((s_tot, tb, ROPE_DIM), F32)],
        grid=(s_tot // sb, 1),
        in_specs=in_specs + [_const_spec(w_qp.shape), _const_spec(w_uk.shape)],
        out_specs=[pl.BlockSpec((N_HEADS, sb, tb, QK_DIM), lambda i, j: (0, i, j, 0)),
                   tile(KV_LORA), tile(ROPE_DIM)],
        compiler_params=pltpu.CompilerParams(dimension_semantics=("arbitrary", "arbitrary"),
                                             vmem_limit_bytes=VMEM_LIMIT),
        name="pre_sample",
    )(x, mod, mod, g_attn, w_a, g_q, w_qn, g_kv, cs, w_qp, w_uk)


def _pre_prompt(x, mod, g_attn, w_a, g_q, w_qn, g_kv, cs, w_qpT, w_ukT, csT, *, tb, tq, tk):
    b, t_tot, _ = x.shape
    assert tb % tq == 0 and tb % tk == 0
    tile, in_specs = _pre_common_specs(1, tb, w_a, w_qn)
    return pl.pallas_call(
        functools.partial(_pre_prompt_kernel, tq=tq, tk=tk),
        out_shape=[jax.ShapeDtypeStruct((b, t_tot // tq, QK_DIM, N_HEADS * tq), BF16),
                   jax.ShapeDtypeStruct((b, t_tot, KV_LORA), F32),
                   jax.ShapeDtypeStruct((b, ROPE_DIM, t_tot), F32),
                   jax.ShapeDtypeStruct((b, t_tot, QK_DIM), BF16),
                   jax.ShapeDtypeStruct((b, t_tot // tk, KV_LORA, tk), BF16)],
        grid=(b, t_tot // tb),
        in_specs=in_specs + [_const_spec(w_qpT.shape), _const_spec(w_ukT.shape),
                             pl.BlockSpec((LANES, tb), lambda i, j: (0, j))],
        out_specs=[pl.BlockSpec((1, tb // tq, QK_DIM, N_HEADS * tq), lambda i, j: (i, j, 0, 0)),
                   tile(KV_LORA), pl.BlockSpec((1, ROPE_DIM, tb), lambda i, j: (i, 0, j)), tile(QK_DIM),
                   pl.BlockSpec((1, tb // tk, KV_LORA, tk), lambda i, j: (i, j, 0, 0))],
        compiler_params=pltpu.CompilerParams(dimension_semantics=("arbitrary", "arbitrary"),
                                             vmem_limit_bytes=VMEM_LIMIT),
        name="pre_prompt",
    )(x, mod, mod, g_attn, w_a, g_q, w_qn, g_kv, cs, w_qpT, w_ukT, csT)


BLOCK_LOOKAHEAD = 3


def _attn_prompt_kernel(qT_ref, k_ref, kT_ref, wuvT_ref, o_ref, m_ref, l_ref, acc_ref, ahead_ref, *,
                        tq, tk, lookahead):
    i = pl.program_id(1)
    per = tq // tk
    m_ref[...] = jnp.full(m_ref.shape, NEG_BIG, F32)
    l_ref[...] = jnp.zeros(l_ref.shape, F32)
    acc_ref[...] = jnp.zeros(acc_ref.shape, F32)
    k_tok = lax.broadcasted_iota(jnp.int32, (tk, tk), 0)
    q_tok = lax.broadcasted_iota(jnp.int32, (tk, tk), 1)
    all_blocks = [(h, e) for h in range(N_HEADS) for e in range(per)]

    def cols(block):
        h, e = block
        return slice(h * tq + e * tk, h * tq + (e + 1) * tk)

    def scores(j, block):
        keys = k_ref[0, pl.ds(pl.multiple_of(j * tk, tk), tk), :]
        return _dot(keys, qT_ref[0, 0, :, cols(block)])

    def run(steps, after):
        pending = [ahead_ref[n] for n in range(lookahead)]
        for n, (j, block, on_diagonal) in enumerate(steps):
            ahead = n + lookahead
            if ahead < len(steps):
                pending.append(scores(*steps[ahead][:2]))
            elif ahead - len(steps) < len(after):
                ahead_ref[ahead - len(steps)] = scores(*after[ahead - len(steps)])
            sT = pending.pop(0)
            if on_diagonal:
                sT = jnp.where(k_tok <= q_tok, sT, NEG_BIG)
            c = cols(block)
            m_prev = m_ref[:, c]
            m_new = jnp.maximum(m_prev, jnp.max(sT, axis=0, keepdims=True))
            alpha = jnp.exp2(m_prev - m_new)
            pT = jnp.exp2(sT - m_new)
            l_ref[:, c] = alpha * l_ref[:, c] + jnp.sum(pT, axis=0, keepdims=True)
            acc_ref[:, c] = alpha * acc_ref[:, c] + _dot(kT_ref[0, j], pT.astype(BF16))
            m_ref[:, c] = m_new

    for n in range(lookahead):
        ahead_ref[n] = scores(0, all_blocks[n])

    def full_chunks(jj, carry):
        first = jj * per
        run([(first + c, block, False) for c in range(per) for block in all_blocks],
            [(first + per, block) for block in all_blocks[:lookahead]])
        return carry

    lax.fori_loop(0, i, full_chunks, 0)

    run([(i * per + c, (h, e), e == c) for c in range(per) for (h, e) in all_blocks if e >= c], [])

    for h in range(N_HEADS):
        head = slice(h * tq, (h + 1) * tq)
        oT = (acc_ref[:, head] * (1.0 / l_ref[:, head])).astype(BF16)
        o_ref[0, :, h * V_DIM:(h + 1) * V_DIM] = _dot(wuvT_ref[h], oT).T


def _attn_prompt(qT, kcat, kT, w_uvT, *, tq, tk):
    b, n_q, _, rows = qT.shape
    assert rows == N_HEADS * tq and tq % tk == 0 and BLOCK_LOOKAHEAD <= tq // tk * N_HEADS
    t = n_q * tq
    return pl.pallas_call(
        functools.partial(_attn_prompt_kernel, tq=tq, tk=tk, lookahead=BLOCK_LOOKAHEAD),
        out_shape=jax.ShapeDtypeStruct((b, t, N_HEADS * V_DIM), F32),
        grid=(b, n_q),
        in_specs=[pl.BlockSpec((1, 1, QK_DIM, rows), lambda bi, i: (bi, i, 0, 0)),
                  pl.BlockSpec((1, t, QK_DIM), lambda bi, i: (bi, 0, 0)),
                  pl.BlockSpec((1, t // tk, KV_LORA, tk), lambda bi, i: (bi, 0, 0, 0)),
                  _const_spec(w_uvT.shape)],
        out_specs=pl.BlockSpec((1, tq, N_HEADS * V_DIM), lambda bi, i: (bi, i, 0)),
        scratch_shapes=[pltpu.VMEM((1, rows), F32), pltpu.VMEM((1, rows), F32),
                        pltpu.VMEM((KV_LORA, rows), F32), pltpu.VMEM((BLOCK_LOOKAHEAD, tk, tk), F32)],
        compiler_params=pltpu.CompilerParams(dimension_semantics=("arbitrary", "arbitrary"),
                                             vmem_limit_bytes=VMEM_LIMIT),
        name="attn_prompt",
    )(qT, kcat, kT, w_uvT)


DMA_UNROLL = 8
SAMPLE_KEY_BLOCKS = 16
SAMPLE_LOOKAHEAD = 3
SEQS_PER_STEP = 4
FETCH_AHEAD = 2


def _attn_sample_kernel(pt_ref, q_ref, nk_ref, nr_ref, wuv_ref, ckv_hbm, krT_hbm, o_ref,
                        kbuf, rbuf, nkbuf, nrbuf, sem, *, n_pages, dec_t, n_blocks, lookahead):
    g = pl.program_id(0)
    n_steps = pl.num_programs(0)
    n_seq = n_steps * SEQS_PER_STEP
    rows = N_HEADS * dec_t
    past = n_pages * PAGE_SIZE

    def page_copies(page, p, slot):
        pos = pl.ds(pl.multiple_of(p * PAGE_SIZE, PAGE_SIZE), PAGE_SIZE)
        return (pltpu.make_async_copy(ckv_hbm.at[page], kbuf.at[slot, pos], sem.at[0, slot]),
                pltpu.make_async_copy(krT_hbm.at[page], rbuf.at[slot, :, pos], sem.at[1, slot]))

    def start_fetch(seq, slot):
        for p in range(n_pages):
            for cp in page_copies(pt_ref[seq, p], p, slot):
                cp.start()

    def wait_fetch(slot):
        def body(p, carry):
            for cp in page_copies(0, p, slot):
                cp.wait()
            return carry
        lax.fori_loop(0, n_pages, body, 0, unroll=DMA_UNROLL)

    def partial_softmax(s, k_lat):
        m_b = jnp.max(s, axis=-1, keepdims=True)
        p = jnp.exp(s - m_b)
        return m_b, jnp.sum(p, axis=-1, keepdims=True), _dot(p.astype(BF16), k_lat)

    def attend(slot):
        q = q_ref[:, slot].reshape(rows, QK_DIM)
        q_lat = q[:, :KV_LORA].astype(BF16)
        q_rope = q[:, KV_LORA:KV_LORA + ROPE_DIM].astype(BF16)

        def block_scores(b):
            ks = slice(b * past // n_blocks, (b + 1) * past // n_blocks)
            k_lat = kbuf[slot, ks, :].astype(BF16)
            return _dot_nt(q_lat, k_lat) + _dot(q_rope, rbuf[slot, :, ks].astype(BF16)), k_lat

        pending = [block_scores(b) for b in range(lookahead)]
        parts = []
        for b in range(n_blocks):
            if b + lookahead < n_blocks:
                pending.append(block_scores(b + lookahead))
            parts.append(partial_softmax(*pending.pop(0)))

        nkbuf[slot, 0:dec_t, :] = nk_ref[slot]
        nrbuf[slot, 0:dec_t, :] = nr_ref[slot]
        n_lat = nkbuf[slot].astype(BF16)
        s_new = _dot_nt(q_lat, n_lat) + _dot_nt(q_rope, nrbuf[slot].astype(BF16))
        q_tok = lax.broadcasted_iota(jnp.int32, (N_HEADS, dec_t, PAGE_SIZE), 1).reshape(rows, PAGE_SIZE)
        k_tok = lax.broadcasted_iota(jnp.int32, (rows, PAGE_SIZE), 1)
        s_new = jnp.where(k_tok <= q_tok, s_new, NEG_BIG)
        parts.append(partial_softmax(s_new, n_lat))

        m = functools.reduce(jnp.maximum, [m_b for m_b, _, _ in parts])
        weights = [jnp.exp(m_b - m) for m_b, _, _ in parts]
        l = functools.reduce(jnp.add, [w_b * l_b for w_b, (_, l_b, _) in zip(weights, parts)])
        acc = functools.reduce(jnp.add, [w_b * acc_b for w_b, (_, _, acc_b) in zip(weights, parts)])

        o = (acc / l).astype(BF16)
        for h in range(N_HEADS):
            o_ref[slot, :, h * V_DIM:(h + 1) * V_DIM] = _dot(o[h * dec_t:(h + 1) * dec_t], wuv_ref[h])

    first = g * SEQS_PER_STEP

    @pl.when(g == 0)
    def _():
        for a in range(FETCH_AHEAD):
            start_fetch(a, a)
        nkbuf[...] = jnp.zeros(nkbuf.shape, F32)
        nrbuf[...] = jnp.zeros(nrbuf.shape, F32)

    for a in range(SEQS_PER_STEP):
        wait_fetch(a)
        start_fetch(jnp.minimum(first + a + FETCH_AHEAD, n_seq - 1), (a + FETCH_AHEAD) % SEQS_PER_STEP)
        attend(a)

    @pl.when(g == n_steps - 1)
    def _():
        for a in range(FETCH_AHEAD):
            wait_fetch(a)


def _attn_sample(page_table, qcat, new_ckv, new_kr, w_uv, cache_ckv, cache_krT):
    _, n_seq, dec_t, _ = qcat.shape
    n_pages = page_table.shape[1]
    past = n_pages * PAGE_SIZE
    per = SEQS_PER_STEP
    assert n_seq % per == 0 and 0 < FETCH_AHEAD < per
    grid_spec = pltpu.PrefetchScalarGridSpec(
        num_scalar_prefetch=1,
        grid=(n_seq // per,),
        in_specs=[pl.BlockSpec((N_HEADS, per, dec_t, QK_DIM), lambda g, pt: (0, g, 0, 0)),
                  pl.BlockSpec((per, dec_t, KV_LORA), lambda g, pt: (g, 0, 0)),
                  pl.BlockSpec((per, dec_t, ROPE_DIM), lambda g, pt: (g, 0, 0)),
                  pl.BlockSpec(w_uv.shape, lambda g, pt: (0, 0, 0), pipeline_mode=pl.Buffered(1)),
                  pl.BlockSpec(memory_space=pl.ANY),
                  pl.BlockSpec(memory_space=pl.ANY)],
        out_specs=pl.BlockSpec((per, dec_t, N_HEADS * V_DIM), lambda g, pt: (g, 0, 0)),
        scratch_shapes=[pltpu.VMEM((per, past, KV_LORA), F32),
                        pltpu.VMEM((per, ROPE_DIM, past), F32),
                        pltpu.VMEM((per, PAGE_SIZE, KV_LORA), F32),
                        pltpu.VMEM((per, PAGE_SIZE, ROPE_DIM), F32),
                        pltpu.SemaphoreType.DMA((2, per))],
    )
    return pl.pallas_call(
        functools.partial(_attn_sample_kernel, n_pages=n_pages, dec_t=dec_t, n_blocks=SAMPLE_KEY_BLOCKS,
                          lookahead=SAMPLE_LOOKAHEAD),
        out_shape=jax.ShapeDtypeStruct((n_seq, dec_t, N_HEADS * V_DIM), F32),
        grid_spec=grid_spec,
        compiler_params=pltpu.CompilerParams(dimension_semantics=("arbitrary",),
                                             vmem_limit_bytes=VMEM_LIMIT),
        name="attn_sample",
    )(page_table, qcat, new_ckv, new_kr, w_uv, cache_ckv, cache_krT)


def _post1_kernel(x_ref, attn_ref, sc_ref, sh_ref, g1_ref, gattn_ref, wbc_ref, cw_ref, wo_ref,
                  prev_ref, x1_ref, nconv_ref, carry_ref, *, sb, tb):
    m = sb * tb

    @pl.when(pl.program_id(1) == 0)
    def _():
        carry_ref[...] = prev_ref[...]

    u = _modulated(x_ref, sc_ref, sh_ref, gattn_ref)
    pb = _dot(u, wbc_ref[:, :3 * D_CONV])
    v = (pb[:, D_CONV:2 * D_CONV] * pb[:, 2 * D_CONV:]).reshape(sb, tb, D_CONV)
    p0 = carry_ref[:, 0:1, :]
    p1 = carry_ref[:, 1:2, :]
    t = lax.broadcasted_iota(jnp.int32, v.shape, 1)
    r1 = pltpu.roll(v, 1, axis=1)
    r2 = pltpu.roll(v, 2, axis=1)
    v1 = jnp.where(t >= 1, r1, p1)
    v2 = jnp.where(t >= 2, r2, jnp.where(t == 1, p1, p0))
    cw = cw_ref[...]
    z = cw[0:1, :] * v2 + cw[1:2, :] * v1 + cw[2:3, :] * v
    conv_out = pb[:, :D_CONV] * z.reshape(m, D_CONV)
    tail = r2[:, 0:CONV_W - 1, :]
    carry_ref[...] = tail
    nconv_ref[...] = tail

    g = jax.nn.sigmoid(_dot(u, wbc_ref[:, 3 * D_CONV:]))
    merged = g[:, :D_MODEL] * attn_ref[...].reshape(m, D_MODEL) + g[:, D_MODEL:] * conv_out
    proj = _dot(merged.astype(BF16), wo_ref[...]).reshape(sb, tb, D_MODEL)
    x1_ref[...] = x_ref[...] + g1_ref[...] * proj


def _post1(x, attn, mod, g_attn, w_bc, conv_w, w_o, prev, *, sb, tb):
    s_tot, t_tot, _ = x.shape
    tile = pl.BlockSpec((sb, tb, D_MODEL), lambda i, j: (i, j, 0))
    mod_spec = lambda chunk: pl.BlockSpec((sb, 1, D_MODEL), lambda i, j: (i, 0, chunk))
    state = pl.BlockSpec((sb, CONV_W - 1, D_CONV), lambda i, j: (i, 0, 0))
    return pl.pallas_call(
        functools.partial(_post1_kernel, sb=sb, tb=tb),
        out_shape=[jax.ShapeDtypeStruct(x.shape, F32),
                   jax.ShapeDtypeStruct((s_tot, CONV_W - 1, D_CONV), F32)],
        grid=(s_tot // sb, t_tot // tb),
        in_specs=[tile, tile, mod_spec(1), mod_spec(0), mod_spec(2),
                  _const_spec((1, D_MODEL)), _const_spec(w_bc.shape),
                  _const_spec(conv_w.shape), _const_spec(w_o.shape), state],
        out_specs=[tile, state],
        scratch_shapes=[pltpu.VMEM((sb, CONV_W - 1, D_CONV), F32)],
        compiler_params=pltpu.CompilerParams(dimension_semantics=("arbitrary", "arbitrary"),
                                             vmem_limit_bytes=VMEM_LIMIT),
        name="post1",
    )(x, attn, mod, mod, mod, g_attn, w_bc, conv_w, w_o, prev)


def _post2_kernel(x1_ref, sc_ref, sh_ref, g2_ref, gmlp_ref, w1_ref, w2_ref, *rest, sb, tb, ff_chunk):
    y_ref = rest[-1]
    m = sb * tb
    u = _modulated(x1_ref, sc_ref, sh_ref, gmlp_ref)
    acc = jnp.zeros((m, D_MODEL), F32)
    for c in range(D_FF // ff_chunk):
        cols = slice(c * ff_chunk, (c + 1) * ff_chunk)
        hdn = jnp.square(jnp.maximum(_dot(u, w1_ref[:, cols]), 0.0))
        acc = acc + _dot(hdn.astype(BF16), w2_ref[cols, :])
    x2 = x1_ref[...] + g2_ref[...] * acc.reshape(sb, tb, D_MODEL)
    y_ref[...] = _rms(x2) * rest[0][...] if len(rest) == 2 else x2


def _post2(x1, mod, g_mlp, w_1, w_2, maybe_g_final, *, sb, tb, ff_chunk):
    s_tot, t_tot, _ = x1.shape
    tile = pl.BlockSpec((sb, tb, D_MODEL), lambda i, j: (i, j, 0))
    mod_spec = lambda chunk: pl.BlockSpec((sb, 1, D_MODEL), lambda i, j: (i, 0, chunk))
    return pl.pallas_call(
        functools.partial(_post2_kernel, sb=sb, tb=tb, ff_chunk=ff_chunk),
        out_shape=jax.ShapeDtypeStruct(x1.shape, F32),
        grid=(s_tot // sb, t_tot // tb),
        in_specs=[tile, mod_spec(4), mod_spec(3), mod_spec(5),
                  _const_spec((1, D_MODEL)), _const_spec(w_1.shape), _const_spec(w_2.shape)]
                 + [_const_spec((1, D_MODEL)) for _ in maybe_g_final],
        out_specs=tile,
        compiler_params=pltpu.CompilerParams(dimension_semantics=("arbitrary", "arbitrary"),
                                             vmem_limit_bytes=VMEM_LIMIT),
        name="post2",
    )(x1, mod, mod, mod, g_mlp, w_1, w_2, *maybe_g_final)


def _rope_table(pos):
    inv = ROPE_THETA ** (-jnp.arange(0, ROPE_DIM, 2, dtype=F32) / ROPE_DIM)
    ang = pos.astype(F32)[:, None] * inv[None, :]
    cos, sin = jnp.cos(ang), jnp.sin(ang)
    return jnp.concatenate([cos, cos, -sin, sin], axis=-1)


def _swap_halves(w):
    half = w.shape[-1] // 2
    return jnp.concatenate([w[..., half:], w[..., :half]], axis=-1)


def _layer_weights(w_in, w_q_b, w_kv_b, w_o, w_1, w_2):
    s1, s2 = Q_LORA + KV_LORA, Q_LORA + KV_LORA + ROPE_DIM
    w_kr = w_in[:, s1:s2]
    wq = w_q_b.reshape(Q_LORA, N_HEADS, NOPE_DIM + ROPE_DIM)
    wq_rope = wq[:, :, NOPE_DIM:]
    wq_pair = jnp.concatenate([wq_rope, _swap_halves(wq_rope)], axis=-1)
    wkv = w_kv_b.reshape(KV_LORA, N_HEADS, NOPE_DIM + V_DIM)
    w_uk, w_uv = wkv[:, :, :NOPE_DIM], wkv[:, :, NOPE_DIM:]
    weights = dict(
        w_a=jnp.concatenate([w_in[:, :s1], w_kr, _swap_halves(w_kr)], axis=1),
        w_bc=w_in[:, s2:],
        w_qn=wq[:, :, :NOPE_DIM].reshape(Q_LORA, N_HEADS * NOPE_DIM),
        w_qp=wq_pair.reshape(Q_LORA, N_HEADS * LANES),
        w_qpT=jnp.transpose(wq_pair, (1, 2, 0)),
        w_uk=jnp.transpose(w_uk, (1, 2, 0)),
        w_ukT=jnp.transpose(w_uk, (1, 0, 2)),
        w_uv=jnp.transpose(w_uv, (1, 0, 2)),
        w_uvT=jnp.transpose(w_uv, (1, 2, 0)),
        w_o=w_o, w_1=w_1, w_2=w_2)
    return {k: v.astype(BF16) for k, v in weights.items()}


PROMPT_TILE = 512
QUERY_TILE = 512
KEY_CHUNK = 256
SAMPLE_SEQS = 64
FF_CHUNK = 1024


def kernel(x_prompt, x_sample, cache_ckv, cache_krope, state_conv, page_table, c_prompt, c_sample,
           w_ada, b_ada, g_attn, w_in, g_q, w_q_b, g_kv, w_kv_b, conv_w, w_o, g_mlp, w_1, w_2, g_final):
    depth = w_in.shape[0]
    batch, seq, _ = x_prompt.shape
    dec_b, dec_t, _ = x_sample.shape
    past_len = page_table.shape[1] * PAGE_SIZE
    cs_p = _rope_table(jnp.arange(seq, dtype=jnp.int32))
    cs_s = _rope_table(past_len + jnp.arange(dec_t, dtype=jnp.int32))
    n_c = batch + dec_b
    c_all = jnp.concatenate([c_prompt, c_sample, jnp.zeros((-n_c % 8, D_MODEL), F32)], axis=0)
    row = lambda g: g.reshape(1, -1)

    hp, hs = x_prompt, x_sample
    outs = [[] for _ in range(6)]
    for l in range(depth):
        w = _layer_weights(w_in[l], w_q_b[l], w_kv_b[l], w_o[l], w_1[l], w_2[l])
        mod = _ada(c_all, w_ada[l], row(b_ada[l]))
        mod_p = mod[:batch].reshape(batch, 1, -1)
        mod_s = mod[batch:n_c].reshape(dec_b, 1, -1)
        shared = (row(g_attn[l]), w["w_a"], row(g_q[l]), w["w_qn"], row(g_kv[l]))

        qT, ckv_p, krT_p, kcat, kT = _pre_prompt(hp, mod_p, *shared, cs_p, w["w_qpT"], w["w_ukT"], cs_p.T,
                                                 tb=PROMPT_TILE, tq=QUERY_TILE, tk=KEY_CHUNK)
        kr_p = jnp.swapaxes(krT_p, 1, 2)
        attn_p = _attn_prompt(qT, kcat, kT, w["w_uvT"], tq=QUERY_TILE, tk=KEY_CHUNK)
        buf0 = jnp.zeros((batch, CONV_W - 1, D_CONV), F32)
        x1_p, conv_p = _post1(hp, attn_p, mod_p, row(g_attn[l]), w["w_bc"], conv_w[l], w["w_o"], buf0,
                              sb=1, tb=PROMPT_TILE)

        qcat_s, ckv_s, kr_s = _pre_sample(hs, mod_s, *shared, cs_s, w["w_qp"], w["w_uk"], sb=SAMPLE_SEQS)
        attn_s = _attn_sample(page_table, qcat_s, ckv_s, kr_s, w["w_uv"], cache_ckv[l],
                              jnp.swapaxes(cache_krope[l], 1, 2))
        x1_s, conv_s = _post1(hs, attn_s, mod_s, row(g_attn[l]), w["w_bc"], conv_w[l], w["w_o"],
                              state_conv[l], sb=SAMPLE_SEQS, tb=dec_t)

        g_fin = (row(g_final),) if l == depth - 1 else ()
        hp = _post2(x1_p, mod_p, row(g_mlp[l]), w["w_1"], w["w_2"], g_fin, sb=1, tb=PROMPT_TILE,
                    ff_chunk=FF_CHUNK)
        hs = _post2(x1_s, mod_s, row(g_mlp[l]), w["w_1"], w["w_2"], g_fin, sb=SAMPLE_SEQS, tb=dec_t,
                    ff_chunk=FF_CHUNK)
        for acc, val in zip(outs, (ckv_p, kr_p, conv_p, ckv_s, kr_s, conv_s)):
            acc.append(val)
    return (hp, hs) + tuple(jnp.stack(o) for o in outs)
```

```python
import functools

import jax
import jax.numpy as jnp
from jax import lax
from jax.experimental import pallas as pl
from jax.experimental.pallas import tpu as pltpu

D_MODEL = 1024
N_HEADS = 8
Q_LORA = 384
KV_LORA = 256
NOPE_DIM = 128
ROPE_DIM = 64
V_DIM = D_MODEL // N_HEADS
D_CONV = D_MODEL
CONV_W = 3
D_FF = 4 * D_MODEL
PAGE_SIZE = 128
ROPE_THETA = 10000.0
EPS = 1e-6
ATTN_SCALE = (NOPE_DIM + ROPE_DIM) ** -0.5
LOG2_E = 1.4426950408889634

LANES = 128
QK_DIM = KV_LORA + LANES
NEG_BIG = -1e30
VMEM_LIMIT = 56 * 1024 * 1024

F32 = jnp.float32
BF16 = jnp.bfloat16


def _const_spec(shape):
    zeros = (0,) * len(shape)
    return pl.BlockSpec(shape, lambda *_: zeros, pipeline_mode=pl.Buffered(1))


def _dot(a, b):
    return jnp.dot(a, b, preferred_element_type=F32)


def _dot_nt(a, b):
    return lax.dot_general(a, b, (((1,), (1,)), ((), ())), preferred_element_type=F32)


def _rms(x):
    return x * lax.rsqrt(jnp.mean(x * x, axis=-1, keepdims=True) + EPS)


def _ada_kernel(c_ref, w_ref, b_ref, o_ref):
    c = c_ref[...]
    a = (c * jax.nn.sigmoid(c)).astype(BF16)
    o_ref[...] = _dot(a, w_ref[...].astype(BF16)) + b_ref[...]


def _ada(c_all, w_ada, b_ada):
    rows = c_all.shape[0]
    n_out = w_ada.shape[1]
    bn = D_MODEL
    return pl.pallas_call(
        _ada_kernel,
        out_shape=jax.ShapeDtypeStruct((rows, n_out), F32),
        grid=(n_out // bn,),
        in_specs=[pl.BlockSpec((rows, D_MODEL), lambda j: (0, 0)),
                  pl.BlockSpec((D_MODEL, bn), lambda j: (0, j)),
                  pl.BlockSpec((1, bn), lambda j: (0, j))],
        out_specs=pl.BlockSpec((rows, bn), lambda j: (0, j)),
        compiler_params=pltpu.CompilerParams(dimension_semantics=("arbitrary",),
                                             vmem_limit_bytes=VMEM_LIMIT),
        name="ada",
    )(c_all, w_ada, b_ada)


def _modulated(x_ref, sc_ref, sh_ref, g_ref):
    u = _rms(x_ref[...]) * g_ref[...] * (1.0 + sc_ref[...]) + sh_ref[...]
    sb, tb, d = u.shape
    return u.reshape(sb * tb, d).astype(BF16)


def _rope_pair(chunk, cs):
    t = chunk * cs
    return t + pltpu.roll(t, LANES // 2, axis=2)


def _latent_and_query(x_ref, sc_ref, sh_ref, gattn_ref, wa_ref, gq_ref, wqn_ref, gkv_ref, cs_ref, ckv_ref):
    sb, tb, _ = x_ref.shape
    u = _modulated(x_ref, sc_ref, sh_ref, gattn_ref)
    pa = _dot(u, wa_ref[...])
    ckv = _rms(pa[:, Q_LORA:Q_LORA + KV_LORA]) * gkv_ref[...]
    ckv_ref[...] = ckv.reshape(sb, tb, KV_LORA)
    kr2 = _rope_pair(pa[:, Q_LORA + KV_LORA:].reshape(sb, tb, LANES), cs_ref[...])
    qn = (_rms(pa[:, :Q_LORA]) * gq_ref[...]).astype(BF16)
    q_nope = _dot(qn, wqn_ref[...])
    return ckv, kr2, qn, q_nope


def _pre_sample_kernel(x_ref, sc_ref, sh_ref, gattn_ref, wa_ref, gq_ref, wqn_ref, gkv_ref, cs_ref,
                       wqp_ref, wuk_ref, qcat_ref, ckv_ref, kr_ref):
    sb, tb, _ = x_ref.shape
    _, kr2, qn, q_nope = _latent_and_query(x_ref, sc_ref, sh_ref, gattn_ref, wa_ref, gq_ref, wqn_ref,
                                           gkv_ref, cs_ref, ckv_ref)
    kr_ref[...] = kr2[:, :, :ROPE_DIM]
    q_pair = _dot(qn, wqp_ref[...])
    for h in range(N_HEADS):
        q_lat = _dot(q_nope[:, h * NOPE_DIM:(h + 1) * NOPE_DIM].astype(BF16), wuk_ref[h]) * ATTN_SCALE
        q_rope = _rope_pair(q_pair[:, h * LANES:(h + 1) * LANES].reshape(sb, tb, LANES), cs_ref[...])
        qcat_ref[h, :, :, 0:KV_LORA] = q_lat.reshape(sb, tb, KV_LORA)
        qcat_ref[h, :, :, KV_LORA:QK_DIM] = q_rope * ATTN_SCALE


def _pre_prompt_kernel(x_ref, sc_ref, sh_ref, gattn_ref, wa_ref, gq_ref, wqn_ref, gkv_ref, cs_ref,
                       wqpT_ref, wukT_ref, csT_ref, qT_ref, ckv_ref, krT_ref, kcat_ref, kT_ref, *, tq, tk):
    _, tb, _ = x_ref.shape
    ckv, kr2, qn, q_nope = _latent_and_query(x_ref, sc_ref, sh_ref, gattn_ref, wa_ref, gq_ref, wqn_ref,
                                             gkv_ref, cs_ref, ckv_ref)
    krT_ref[0] = kr2.reshape(tb, LANES).T[:ROPE_DIM]
    lane = lax.broadcasted_iota(jnp.int32, kr2.shape, 2)
    kcat_ref[:, :, 0:KV_LORA] = ckv.reshape(1, tb, KV_LORA).astype(BF16)
    kcat_ref[:, :, KV_LORA:QK_DIM] = jnp.where(lane < ROPE_DIM, kr2, 0.0).astype(BF16)
    ckvT = ckv.T.astype(BF16)
    csT = csT_ref[...]
    scale = ATTN_SCALE * LOG2_E
    for a in range(tb // tk):
        kT_ref[0, a] = ckvT[:, a * tk:(a + 1) * tk]
    for h in range(N_HEADS):
        q_latT = _dot_nt(wukT_ref[h], q_nope[:, h * NOPE_DIM:(h + 1) * NOPE_DIM].astype(BF16))
        t = _dot_nt(wqpT_ref[h], qn) * csT
        q_ropeT = t + jnp.concatenate([t[LANES // 2:], t[:LANES // 2]], axis=0)
        for a in range(tb // tq):
            cols = slice(a * tq, (a + 1) * tq)
            qT_ref[0, a, 0:KV_LORA, h * tq:(h + 1) * tq] = (q_latT[:, cols] * scale).astype(BF16)
            qT_ref[0, a, KV_LORA:QK_DIM, h * tq:(h + 1) * tq] = (q_ropeT[:, cols] * scale).astype(BF16)


def _pre_common_specs(sb, tb, w_a, w_qn):
    tile = lambda w: pl.BlockSpec((sb, tb, w), lambda i, j: (i, j, 0))
    mod_spec = lambda chunk: pl.BlockSpec((sb, 1, D_MODEL), lambda i, j: (i, 0, chunk))
    in_specs = [tile(D_MODEL), mod_spec(1), mod_spec(0),
                _const_spec((1, D_MODEL)), _const_spec(w_a.shape), _const_spec((1, Q_LORA)),
                _const_spec(w_qn.shape), _const_spec((1, KV_LORA)),
                pl.BlockSpec((tb, LANES), lambda i, j: (j, 0))]
    return tile, in_specs


def _pre_sample(x, mod, g_attn, w_a, g_q, w_qn, g_kv, cs, w_qp, w_uk, *, sb):
    s_tot, tb, _ = x.shape
    tile, in_specs = _pre_common_specs(sb, tb, w_a, w_qn)
    return pl.pallas_call(
        _pre_sample_kernel,
        out_shape=[jax.ShapeDtypeStruct((N_HEADS, s_tot, tb, QK_DIM), F32),
                   jax.ShapeDtypeStruct((s_tot, tb, KV_LORA), F32),
                   jax.ShapeDtypeStruct((s_tot, tb, ROPE_DIM), F32)],
        grid=(s_tot // sb, 1),
        in_specs=in_specs + [_const_spec(w_qp.shape), _const_spec(w_uk.shape)],
        out_specs=[pl.BlockSpec((N_HEADS, sb, tb, QK_DIM), lambda i, j: (0, i, j, 0)),
                   tile(KV_LORA), tile(ROPE_DIM)],
        compiler_params=pltpu.CompilerParams(dimension_semantics=("arbitrary", "arbitrary"),
                                             vmem_limit_bytes=VMEM_LIMIT),
        name="pre_sample",
    )(x, mod, mod, g_attn, w_a, g_q, w_qn, g_kv, cs, w_qp, w_uk)


def _pre_prompt(x, mod, g_attn, w_a, g_q, w_qn, g_kv, cs, w_qpT, w_ukT, csT, *, tb, tq, tk):
    b, t_tot, _ = x.shape
    assert tb % tq == 0 and tb % tk == 0
    tile, in_specs = _pre_common_specs(1, tb, w_a, w_qn)
    return pl.pallas_call(
        functools.partial(_pre_prompt_kernel, tq=tq, tk=tk),
        out_shape=[jax.ShapeDtypeStruct((b, t_tot // tq, QK_DIM, N_HEADS * tq), BF16),
                   jax.ShapeDtypeStruct((b, t_tot, KV_LORA), F32),
                   jax.ShapeDtypeStruct((b, ROPE_DIM, t_tot), F32),
                   jax.ShapeDtypeStruct((b, t_tot, QK_DIM), BF16),
                   jax.ShapeDtypeStruct((b, t_tot // tk, KV_LORA, tk), BF16)],
        grid=(b, t_tot // tb),
        in_specs=in_specs + [_const_spec(w_qpT.shape), _const_spec(w_ukT.shape),
                             pl.BlockSpec((LANES, tb), lambda i, j: (0, j))],
        out_specs=[pl.BlockSpec((1, tb // tq, QK_DIM, N_HEADS * tq), lambda i, j: (i, j, 0, 0)),
                   tile(KV_LORA), pl.BlockSpec((1, ROPE_DIM, tb), lambda i, j: (i, 0, j)), tile(QK_DIM),
                   pl.BlockSpec((1, tb // tk, KV_LORA, tk), lambda i, j: (i, j, 0, 0))],
        compiler_params=pltpu.CompilerParams(dimension_semantics=("arbitrary", "arbitrary"),
                                             vmem_limit_bytes=VMEM_LIMIT),
        name="pre_prompt",
    )(x, mod, mod, g_attn, w_a, g_q, w_qn, g_kv, cs, w_qpT, w_ukT, csT)


BLOCK_LOOKAHEAD = 3


def _attn_prompt_kernel(qT_ref, k_ref, kT_ref, wuvT_ref, o_ref, m_ref, l_ref, acc_ref, ahead_ref, *,
                        tq, tk, lookahead):
    i = pl.program_id(1)
    per = tq // tk
    m_ref[...] = jnp.full(m_ref.shape, NEG_BIG, F32)
    l_ref[...] = jnp.zeros(l_ref.shape, F32)
    acc_ref[...] = jnp.zeros(acc_ref.shape, F32)
    k_tok = lax.broadcasted_iota(jnp.int32, (tk, tk), 0)
    q_tok = lax.broadcasted_iota(jnp.int32, (tk, tk), 1)
    all_blocks = [(h, e) for h in range(N_HEADS) for e in range(per)]

    def cols(block):
        h, e = block
        return slice(h * tq + e * tk, h * tq + (e + 1) * tk)

    def scores(j, block):
        keys = k_ref[0, pl.ds(pl.multiple_of(j * tk, tk), tk), :]
        return _dot(keys, qT_ref[0, 0, :, cols(block)])

    def run(steps, after):
        pending = [ahead_ref[n] for n in range(lookahead)]
        for n, (j, block, on_diagonal) in enumerate(steps):
            ahead = n + lookahead
            if ahead < len(steps):
                pending.append(scores(*steps[ahead][:2]))
            elif ahead - len(steps) < len(after):
                ahead_ref[ahead - len(steps)] = scores(*after[ahead - len(steps)])
            sT = pending.pop(0)
            if on_diagonal:
                sT = jnp.where(k_tok <= q_tok, sT, NEG_BIG)
            c = cols(block)
            m_prev = m_ref[:, c]
            m_new = jnp.maximum(m_prev, jnp.max(sT, axis=0, keepdims=True))
            alpha = jnp.exp2(m_prev - m_new)
            pT = jnp.exp2(sT - m_new)
            l_ref[:, c] = alpha * l_ref[:, c] + jnp.sum(pT, axis=0, keepdims=True)
            acc_ref[:, c] = alpha * acc_ref[:, c] + _dot(kT_ref[0, j], pT.astype(BF16))
            m_ref[:, c] = m_new

    for n in range(lookahead):
        ahead_ref[n] = scores(0, all_blocks[n])

    def full_chunks(jj, carry):
        first = jj * per
        run([(first + c, block, False) for c in range(per) for block in all_blocks],
            [(first + per, block) for block in all_blocks[:lookahead]])
        return carry

    lax.fori_loop(0, i, full_chunks, 0)

    run([(i * per + c, (h, e), e == c) for c in range(per) for (h, e) in all_blocks if e >= c], [])

    for h in range(N_HEADS):
        head = slice(h * tq, (h + 1) * tq)
        oT = (acc_ref[:, head] * (1.0 / l_ref[:, head])).astype(BF16)
        o_ref[0, :, h * V_DIM:(h + 1) * V_DIM] = _dot(wuvT_ref[h], oT).T


def _attn_prompt(qT, kcat, kT, w_uvT, *, tq, tk):
    b, n_q, _, rows = qT.shape
    assert rows == N_HEADS * tq and tq % tk == 0 and BLOCK_LOOKAHEAD <= tq // tk * N_HEADS
    t = n_q * tq
    return pl.pallas_call(
        functools.partial(_attn_prompt_kernel, tq=tq, tk=tk, lookahead=BLOCK_LOOKAHEAD),
        out_shape=jax.ShapeDtypeStruct((b, t, N_HEADS * V_DIM), F32),
        grid=(b, n_q),
        in_specs=[pl.BlockSpec((1, 1, QK_DIM, rows), lambda bi, i: (bi, i, 0, 0)),
                  pl.BlockSpec((1, t, QK_DIM), lambda bi, i: (bi, 0, 0)),
                  pl.BlockSpec((1, t // tk, KV_LORA, tk), lambda bi, i: (bi, 0, 0, 0)),
                  _const_spec(w_uvT.shape)],
        out_specs=pl.BlockSpec((1, tq, N_HEADS * V_DIM), lambda bi, i: (bi, i, 0)),
        scratch_shapes=[pltpu.VMEM((1, rows), F32), pltpu.VMEM((1, rows), F32),
                        pltpu.VMEM((KV_LORA, rows), F32), pltpu.VMEM((BLOCK_LOOKAHEAD, tk, tk), F32)],
        compiler_params=pltpu.CompilerParams(dimension_semantics=("arbitrary", "arbitrary"),
                                             vmem_limit_bytes=VMEM_LIMIT),
        name="attn_prompt",
    )(qT, kcat, kT, w_uvT)


DMA_UNROLL = 8
SAMPLE_KEY_BLOCKS = 16
SAMPLE_LOOKAHEAD = 3
SEQS_PER_STEP = 4
FETCH_AHEAD = 2


def _attn_sample_kernel(pt_ref, q_ref, nk_ref, nr_ref, wuv_ref, ckv_hbm, krT_hbm, o_ref,
                        kbuf, rbuf, nkbuf, nrbuf, sem, *, n_pages, dec_t, n_blocks, lookahead):
    g = pl.program_id(0)
    n_steps = pl.num_programs(0)
    n_seq = n_steps * SEQS_PER_STEP
    rows = N_HEADS * dec_t
    past = n_pages * PAGE_SIZE

    def page_copies(page, p, slot):
        pos = pl.ds(pl.multiple_of(p * PAGE_SIZE, PAGE_SIZE), PAGE_SIZE)
        return (pltpu.make_async_copy(ckv_hbm.at[page], kbuf.at[slot, pos], sem.at[0, slot]),
                pltpu.make_async_copy(krT_hbm.at[page], rbuf.at[slot, p], sem.at[1, slot]))

    def start_fetch(seq, slot):
        for p in range(n_pages):
            for cp in page_copies(pt_ref[seq, p], p, slot):
                cp.start()

    def wait_fetch(slot):
        def body(p, carry):
            for cp in page_copies(0, p, slot):
                cp.wait()
            return carry
        lax.fori_loop(0, n_pages, body, 0, unroll=DMA_UNROLL)

    def partial_softmax(s, k_lat):
        m_b = jnp.max(s, axis=-1, keepdims=True)
        p = jnp.exp(s - m_b)
        return m_b, jnp.sum(p, axis=-1, keepdims=True), _dot(p.astype(BF16), k_lat)

    def attend(slot):
        q = q_ref[:, slot].reshape(rows, QK_DIM)
        q_lat = q[:, :KV_LORA].astype(BF16)
        q_rope = q[:, KV_LORA:KV_LORA + ROPE_DIM].astype(BF16)

        def block_scores(b):
            ks = slice(b * past // n_blocks, (b + 1) * past // n_blocks)
            k_lat = kbuf[slot, ks, :].astype(BF16)
            pages = range(b * n_pages // n_blocks, (b + 1) * n_pages // n_blocks)
            s_rope = jnp.concatenate([_dot(q_rope, rbuf[slot, p].astype(BF16)) for p in pages], axis=1)
            return _dot_nt(q_lat, k_lat) + s_rope, k_lat

        pending = [block_scores(b) for b in range(lookahead)]
        parts = []
        for b in range(n_blocks):
            if b + lookahead < n_blocks:
                pending.append(block_scores(b + lookahead))
            parts.append(partial_softmax(*pending.pop(0)))

        nkbuf[slot, 0:dec_t, :] = nk_ref[slot]
        nrbuf[slot, 0:dec_t, :] = nr_ref[slot]
        n_lat = nkbuf[slot].astype(BF16)
        s_new = _dot_nt(q_lat, n_lat) + _dot_nt(q_rope, nrbuf[slot].astype(BF16))
        q_tok = lax.broadcasted_iota(jnp.int32, (N_HEADS, dec_t, PAGE_SIZE), 1).reshape(rows, PAGE_SIZE)
        k_tok = lax.broadcasted_iota(jnp.int32, (rows, PAGE_SIZE), 1)
        s_new = jnp.where(k_tok <= q_tok, s_new, NEG_BIG)
        parts.append(partial_softmax(s_new, n_lat))

        m = functools.reduce(jnp.maximum, [m_b for m_b, _, _ in parts])
        weights = [jnp.exp(m_b - m) for m_b, _, _ in parts]
        l = functools.reduce(jnp.add, [w_b * l_b for w_b, (_, l_b, _) in zip(weights, parts)])
        acc = functools.reduce(jnp.add, [w_b * acc_b for w_b, (_, _, acc_b) in zip(weights, parts)])

        o = (acc / l).astype(BF16)
        for h in range(N_HEADS):
            o_ref[slot, :, h * V_DIM:(h + 1) * V_DIM] = _dot(o[h * dec_t:(h + 1) * dec_t], wuv_ref[h])

    first = g * SEQS_PER_STEP

    @pl.when(g == 0)
    def _():
        for a in range(FETCH_AHEAD):
            start_fetch(a, a)
        nkbuf[...] = jnp.zeros(nkbuf.shape, F32)
        nrbuf[...] = jnp.zeros(nrbuf.shape, F32)

    for a in range(SEQS_PER_STEP):
        wait_fetch(a)
        start_fetch(jnp.minimum(first + a + FETCH_AHEAD, n_seq - 1), (a + FETCH_AHEAD) % SEQS_PER_STEP)
        attend(a)

    @pl.when(g == n_steps - 1)
    def _():
        for a in range(FETCH_AHEAD):
            wait_fetch(a)


def _attn_sample(page_table, qcat, new_ckv, new_kr, w_uv, cache_ckv, cache_krT):
    _, n_seq, dec_t, _ = qcat.shape
    n_pages = page_table.shape[1]
    past = n_pages * PAGE_SIZE
    per = SEQS_PER_STEP
    assert n_seq % per == 0 and 0 < FETCH_AHEAD < per
    grid_spec = pltpu.PrefetchScalarGridSpec(
        num_scalar_prefetch=1,
        grid=(n_seq // per,),
        in_specs=[pl.BlockSpec((N_HEADS, per, dec_t, QK_DIM), lambda g, pt: (0, g, 0, 0)),
                  pl.BlockSpec((per, dec_t, KV_LORA), lambda g, pt: (g, 0, 0)),
                  pl.BlockSpec((per, dec_t, ROPE_DIM), lambda g, pt: (g, 0, 0)),
                  pl.BlockSpec(w_uv.shape, lambda g, pt: (0, 0, 0), pipeline_mode=pl.Buffered(1)),
                  pl.BlockSpec(memory_space=pl.ANY),
                  pl.BlockSpec(memory_space=pl.ANY)],
        out_specs=pl.BlockSpec((per, dec_t, N_HEADS * V_DIM), lambda g, pt: (g, 0, 0)),
        scratch_shapes=[pltpu.VMEM((per, past, KV_LORA), F32),
                        pltpu.VMEM((per, n_pages, ROPE_DIM, PAGE_SIZE), F32),
                        pltpu.VMEM((per, PAGE_SIZE, KV_LORA), F32),
                        pltpu.VMEM((per, PAGE_SIZE, ROPE_DIM), F32),
                        pltpu.SemaphoreType.DMA((2, per))],
    )
    return pl.pallas_call(
        functools.partial(_attn_sample_kernel, n_pages=n_pages, dec_t=dec_t, n_blocks=SAMPLE_KEY_BLOCKS,
                          lookahead=SAMPLE_LOOKAHEAD),
        out_shape=jax.ShapeDtypeStruct((n_seq, dec_t, N_HEADS * V_DIM), F32),
        grid_spec=grid_spec,
        compiler_params=pltpu.CompilerParams(dimension_semantics=("arbitrary",),
                                             vmem_limit_bytes=VMEM_LIMIT),
        name="attn_sample",
    )(page_table, qcat, new_ckv, new_kr, w_uv, cache_ckv, cache_krT)


def _post1_kernel(x_ref, attn_ref, sc_ref, sh_ref, g1_ref, gattn_ref, wbc_ref, cw_ref, wo_ref,
                  prev_ref, x1_ref, nconv_ref, carry_ref, *, sb, tb, n_split):
    ts = tb // n_split
    m = sb * ts

    @pl.when(pl.program_id(1) == 0)
    def _():
        carry_ref[...] = prev_ref[...]

    def project(rows):
        x = x_ref[:, rows, :]
        u = _rms(x) * gattn_ref[...] * (1.0 + sc_ref[...]) + sh_ref[...]
        u = u.reshape(m, D_MODEL).astype(BF16)
        return x, _dot(u, wbc_ref[:, :3 * D_CONV]), _dot(u, wbc_ref[:, 3 * D_CONV:])

    def finish(rows, x, pb, pc):
        v = (pb[:, D_CONV:2 * D_CONV] * pb[:, 2 * D_CONV:]).reshape(sb, ts, D_CONV)
        p0 = carry_ref[:, 0:1, :]
        p1 = carry_ref[:, 1:2, :]
        t = lax.broadcasted_iota(jnp.int32, v.shape, 1)
        r1 = pltpu.roll(v, 1, axis=1)
        r2 = pltpu.roll(v, 2, axis=1)
        v1 = jnp.where(t >= 1, r1, p1)
        v2 = jnp.where(t >= 2, r2, jnp.where(t == 1, p1, p0))
        cw = cw_ref[...]
        z = cw[0:1, :] * v2 + cw[1:2, :] * v1 + cw[2:3, :] * v
        conv_out = pb[:, :D_CONV] * z.reshape(m, D_CONV)
        carry_ref[...] = r2[:, 0:CONV_W - 1, :]

        g = jax.nn.sigmoid(pc)
        merged = g[:, :D_MODEL] * attn_ref[:, rows, :].reshape(m, D_MODEL) + g[:, D_MODEL:] * conv_out
        proj = _dot(merged.astype(BF16), wo_ref[...]).reshape(sb, ts, D_MODEL)
        x1_ref[:, rows, :] = x + g1_ref[...] * proj

    ranges = [slice(r * ts, (r + 1) * ts) for r in range(n_split)]
    projected = [project(rows) for rows in ranges]
    for rows, args in zip(ranges, projected):
        finish(rows, *args)
    nconv_ref[...] = carry_ref[...]


def _post1(x, attn, mod, g_attn, w_bc, conv_w, w_o, prev, *, sb, tb, n_split):
    s_tot, t_tot, _ = x.shape
    tile = pl.BlockSpec((sb, tb, D_MODEL), lambda i, j: (i, j, 0))
    mod_spec = lambda chunk: pl.BlockSpec((sb, 1, D_MODEL), lambda i, j: (i, 0, chunk))
    state = pl.BlockSpec((sb, CONV_W - 1, D_CONV), lambda i, j: (i, 0, 0))
    return pl.pallas_call(
        functools.partial(_post1_kernel, sb=sb, tb=tb, n_split=n_split),
        out_shape=[jax.ShapeDtypeStruct(x.shape, F32),
                   jax.ShapeDtypeStruct((s_tot, CONV_W - 1, D_CONV), F32)],
        grid=(s_tot // sb, t_tot // tb),
        in_specs=[tile, tile, mod_spec(1), mod_spec(0), mod_spec(2),
                  _const_spec((1, D_MODEL)), _const_spec(w_bc.shape),
                  _const_spec(conv_w.shape), _const_spec(w_o.shape), state],
        out_specs=[tile, state],
        scratch_shapes=[pltpu.VMEM((sb, CONV_W - 1, D_CONV), F32)],
        compiler_params=pltpu.CompilerParams(dimension_semantics=("arbitrary", "arbitrary"),
                                             vmem_limit_bytes=VMEM_LIMIT),
        name="post1",
    )(x, attn, mod, mod, mod, g_attn, w_bc, conv_w, w_o, prev)


def _post2_kernel(x1_ref, sc_ref, sh_ref, g2_ref, gmlp_ref, w1_ref, w2_ref, *rest, sb, tb, ff_chunk):
    y_ref = rest[-1]
    m = sb * tb
    u = _modulated(x1_ref, sc_ref, sh_ref, gmlp_ref)
    acc = jnp.zeros((m, D_MODEL), F32)
    for c in range(D_FF // ff_chunk):
        cols = slice(c * ff_chunk, (c + 1) * ff_chunk)
        hdn = jnp.square(jnp.maximum(_dot(u, w1_ref[:, cols]), 0.0))
        acc = acc + _dot(hdn.astype(BF16), w2_ref[cols, :])
    x2 = x1_ref[...] + g2_ref[...] * acc.reshape(sb, tb, D_MODEL)
    y_ref[...] = _rms(x2) * rest[0][...] if len(rest) == 2 else x2


def _post2(x1, mod, g_mlp, w_1, w_2, maybe_g_final, *, sb, tb, ff_chunk):
    s_tot, t_tot, _ = x1.shape
    tile = pl.BlockSpec((sb, tb, D_MODEL), lambda i, j: (i, j, 0))
    mod_spec = lambda chunk: pl.BlockSpec((sb, 1, D_MODEL), lambda i, j: (i, 0, chunk))
    return pl.pallas_call(
        functools.partial(_post2_kernel, sb=sb, tb=tb, ff_chunk=ff_chunk),
        out_shape=jax.ShapeDtypeStruct(x1.shape, F32),
        grid=(s_tot // sb, t_tot // tb),
        in_specs=[tile, mod_spec(4), mod_spec(3), mod_spec(5),
                  _const_spec((1, D_MODEL)), _const_spec(w_1.shape), _const_spec(w_2.shape)]
                 + [_const_spec((1, D_MODEL)) for _ in maybe_g_final],
        out_specs=tile,
        compiler_params=pltpu.CompilerParams(dimension_semantics=("arbitrary", "arbitrary"),
                                             vmem_limit_bytes=VMEM_LIMIT),
        name="post2",
    )(x1, mod, mod, mod, g_mlp, w_1, w_2, *maybe_g_final)


def _rope_table(pos):
    inv = ROPE_THETA ** (-jnp.arange(0, ROPE_DIM, 2, dtype=F32) / ROPE_DIM)
    ang = pos.astype(F32)[:, None] * inv[None, :]
    cos, sin = jnp.cos(ang), jnp.sin(ang)
    return jnp.concatenate([cos, cos, -sin, sin], axis=-1)


def _swap_halves(w):
    half = w.shape[-1] // 2
    return jnp.concatenate([w[..., half:], w[..., :half]], axis=-1)


def _layer_weights(w_in, w_q_b, w_kv_b, w_o, w_1, w_2):
    s1, s2 = Q_LORA + KV_LORA, Q_LORA + KV_LORA + ROPE_DIM
    w_kr = w_in[:, s1:s2]
    wq = w_q_b.reshape(Q_LORA, N_HEADS, NOPE_DIM + ROPE_DIM)
    wq_rope = wq[:, :, NOPE_DIM:]
    wq_pair = jnp.concatenate([wq_rope, _swap_halves(wq_rope)], axis=-1)
    wkv = w_kv_b.reshape(KV_LORA, N_HEADS, NOPE_DIM + V_DIM)
    w_uk, w_uv = wkv[:, :, :NOPE_DIM], wkv[:, :, NOPE_DIM:]
    weights = dict(
        w_a=jnp.concatenate([w_in[:, :s1], w_kr, _swap_halves(w_kr)], axis=1),
        w_bc=w_in[:, s2:],
        w_qn=wq[:, :, :NOPE_DIM].reshape(Q_LORA, N_HEADS * NOPE_DIM),
        w_qp=wq_pair.reshape(Q_LORA, N_HEADS * LANES),
        w_qpT=jnp.transpose(wq_pair, (1, 2, 0)),
        w_uk=jnp.transpose(w_uk, (1, 2, 0)),
        w_ukT=jnp.transpose(w_uk, (1, 0, 2)),
        w_uv=jnp.transpose(w_uv, (1, 0, 2)),
        w_uvT=jnp.transpose(w_uv, (1, 2, 0)),
        w_o=w_o, w_1=w_1, w_2=w_2)
    return {k: v.astype(BF16) for k, v in weights.items()}


PROMPT_TILE = 512
PROMPT_SPLIT = 2
QUERY_TILE = 512
KEY_CHUNK = 256
SAMPLE_SEQS = 64
FF_CHUNK = 1024


def kernel(x_prompt, x_sample, cache_ckv, cache_krope, state_conv, page_table, c_prompt, c_sample,
           w_ada, b_ada, g_attn, w_in, g_q, w_q_b, g_kv, w_kv_b, conv_w, w_o, g_mlp, w_1, w_2, g_final):
    depth = w_in.shape[0]
    batch, seq, _ = x_prompt.shape
    dec_b, dec_t, _ = x_sample.shape
    past_len = page_table.shape[1] * PAGE_SIZE
    cs_p = _rope_table(jnp.arange(seq, dtype=jnp.int32))
    cs_s = _rope_table(past_len + jnp.arange(dec_t, dtype=jnp.int32))
    n_c = batch + dec_b
    c_all = jnp.concatenate([c_prompt, c_sample, jnp.zeros((-n_c % 8, D_MODEL), F32)], axis=0)
    row = lambda g: g.reshape(1, -1)

    hp, hs = x_prompt, x_sample
    outs = [[] for _ in range(6)]
    for l in range(depth):
        w = _layer_weights(w_in[l], w_q_b[l], w_kv_b[l], w_o[l], w_1[l], w_2[l])
        mod = _ada(c_all, w_ada[l], row(b_ada[l]))
        mod_p = mod[:batch].reshape(batch, 1, -1)
        mod_s = mod[batch:n_c].reshape(dec_b, 1, -1)
        shared = (row(g_attn[l]), w["w_a"], row(g_q[l]), w["w_qn"], row(g_kv[l]))

        qT, ckv_p, krT_p, kcat, kT = _pre_prompt(hp, mod_p, *shared, cs_p, w["w_qpT"], w["w_ukT"], cs_p.T,
                                                 tb=PROMPT_TILE, tq=QUERY_TILE, tk=KEY_CHUNK)
        kr_p = jnp.swapaxes(krT_p, 1, 2)
        attn_p = _attn_prompt(qT, kcat, kT, w["w_uvT"], tq=QUERY_TILE, tk=KEY_CHUNK)
        buf0 = jnp.zeros((batch, CONV_W - 1, D_CONV), F32)
        x1_p, conv_p = _post1(hp, attn_p, mod_p, row(g_attn[l]), w["w_bc"], conv_w[l], w["w_o"], buf0,
                              sb=1, tb=PROMPT_TILE, n_split=PROMPT_SPLIT)

        qcat_s, ckv_s, kr_s = _pre_sample(hs, mod_s, *shared, cs_s, w["w_qp"], w["w_uk"], sb=SAMPLE_SEQS)
        attn_s = _attn_sample(page_table, qcat_s, ckv_s, kr_s, w["w_uv"], cache_ckv[l],
                              jnp.swapaxes(cache_krope[l], 1, 2))
        x1_s, conv_s = _post1(hs, attn_s, mod_s, row(g_attn[l]), w["w_bc"], conv_w[l], w["w_o"],
                              state_conv[l], sb=SAMPLE_SEQS, tb=dec_t, n_split=1)

        g_fin = (row(g_final),) if l == depth - 1 else ()
        hp = _post2(x1_p, mod_p, row(g_mlp[l]), w["w_1"], w["w_2"], g_fin, sb=1, tb=PROMPT_TILE,
                    ff_chunk=FF_CHUNK)
        hs = _post2(x1_s, mod_s, row(g_mlp[l]), w["w_1"], w["w_2"], g_fin, sb=SAMPLE_SEQS, tb=dec_t,
                    ff_chunk=FF_CHUNK)
        for acc, val in zip(outs, (ckv_p, kr_p, conv_p, ckv_s, kr_s, conv_s)):
            acc.append(val)
    return (hp, hs) + tuple(jnp.stack(o) for o in outs)
```

```python
import functools

import jax
import jax.numpy as jnp
from jax import lax
from jax.experimental import pallas as pl
from jax.experimental.pallas import tpu as pltpu

D_MODEL = 1024
N_HEADS = 8
Q_LORA = 384
KV_LORA = 256
NOPE_DIM = 128
ROPE_DIM = 64
V_DIM = D_MODEL // N_HEADS
D_CONV = D_MODEL
CONV_W = 3
D_FF = 4 * D_MODEL
PAGE_SIZE = 128
ROPE_THETA = 10000.0
EPS = 1e-6
ATTN_SCALE = (NOPE_DIM + ROPE_DIM) ** -0.5
LOG2_E = 1.4426950408889634

LANES = 128
QK_DIM = KV_LORA + LANES
NEG_BIG = -1e30
VMEM_LIMIT = 56 * 1024 * 1024

F32 = jnp.float32
BF16 = jnp.bfloat16


def _const_spec(shape):
    zeros = (0,) * len(shape)
    return pl.BlockSpec(shape, lambda *_: zeros, pipeline_mode=pl.Buffered(1))


def _dot(a, b):
    return jnp.dot(a, b, preferred_element_type=F32)


def _dot_nt(a, b):
    return lax.dot_general(a, b, (((1,), (1,)), ((), ())), preferred_element_type=F32)


def _rms(x):
    return x * lax.rsqrt(jnp.mean(x * x, axis=-1, keepdims=True) + EPS)


def _ada_kernel(c_ref, w_ref, b_ref, o_ref):
    c = c_ref[...]
    a = (c * jax.nn.sigmoid(c)).astype(BF16)
    o_ref[...] = _dot(a, w_ref[...].astype(BF16)) + b_ref[...]


def _ada(c_all, w_ada, b_ada):
    rows = c_all.shape[0]
    n_out = w_ada.shape[1]
    bn = D_MODEL
    return pl.pallas_call(
        _ada_kernel,
        out_shape=jax.ShapeDtypeStruct((rows, n_out), F32),
        grid=(n_out // bn,),
        in_specs=[pl.BlockSpec((rows, D_MODEL), lambda j: (0, 0)),
                  pl.BlockSpec((D_MODEL, bn), lambda j: (0, j)),
                  pl.BlockSpec((1, bn), lambda j: (0, j))],
        out_specs=pl.BlockSpec((rows, bn), lambda j: (0, j)),
        compiler_params=pltpu.CompilerParams(dimension_semantics=("arbitrary",),
                                             vmem_limit_bytes=VMEM_LIMIT),
        name="ada",
    )(c_all, w_ada, b_ada)


def _modulated(x_ref, sc_ref, sh_ref, g_ref):
    u = _rms(x_ref[...]) * g_ref[...] * (1.0 + sc_ref[...]) + sh_ref[...]
    sb, tb, d = u.shape
    return u.reshape(sb * tb, d).astype(BF16)


def _rope_pair(chunk, cs):
    t = chunk * cs
    return t + pltpu.roll(t, LANES // 2, axis=2)


def _latent_and_query(x_ref, sc_ref, sh_ref, gattn_ref, wa_ref, gq_ref, wqn_ref, gkv_ref, cs_ref, ckv_ref):
    sb, tb, _ = x_ref.shape
    u = _modulated(x_ref, sc_ref, sh_ref, gattn_ref)
    pa = _dot(u, wa_ref[...])
    ckv = _rms(pa[:, Q_LORA:Q_LORA + KV_LORA]) * gkv_ref[...]
    ckv_ref[...] = ckv.reshape(sb, tb, KV_LORA)
    kr2 = _rope_pair(pa[:, Q_LORA + KV_LORA:].reshape(sb, tb, LANES), cs_ref[...])
    qn = (_rms(pa[:, :Q_LORA]) * gq_ref[...]).astype(BF16)
    q_nope = _dot(qn, wqn_ref[...])
    return ckv, kr2, qn, q_nope


def _pre_sample_kernel(x_ref, sc_ref, sh_ref, gattn_ref, wa_ref, gq_ref, wqn_ref, gkv_ref, cs_ref,
                       wqp_ref, wuk_ref, qcat_ref, ckv_ref, kr_ref):
    sb, tb, _ = x_ref.shape
    _, kr2, qn, q_nope = _latent_and_query(x_ref, sc_ref, sh_ref, gattn_ref, wa_ref, gq_ref, wqn_ref,
                                           gkv_ref, cs_ref, ckv_ref)
    kr_ref[...] = kr2[:, :, :ROPE_DIM]
    q_pair = _dot(qn, wqp_ref[...])
    for h in range(N_HEADS):
        q_lat = _dot(q_nope[:, h * NOPE_DIM:(h + 1) * NOPE_DIM].astype(BF16), wuk_ref[h]) * ATTN_SCALE
        q_rope = _rope_pair(q_pair[:, h * LANES:(h + 1) * LANES].reshape(sb, tb, LANES), cs_ref[...])
        qcat_ref[h, :, :, 0:KV_LORA] = q_lat.reshape(sb, tb, KV_LORA)
        qcat_ref[h, :, :, KV_LORA:QK_DIM] = q_rope * ATTN_SCALE


def _pre_prompt_kernel(x_ref, sc_ref, sh_ref, gattn_ref, wa_ref, gq_ref, wqn_ref, gkv_ref, cs_ref,
                       wqpT_ref, wukT_ref, csT_ref, qT_ref, ckv_ref, krT_ref, kcat_ref, kT_ref, *, tq, tk):
    _, tb, _ = x_ref.shape
    ckv, kr2, qn, q_nope = _latent_and_query(x_ref, sc_ref, sh_ref, gattn_ref, wa_ref, gq_ref, wqn_ref,
                                             gkv_ref, cs_ref, ckv_ref)
    krT_ref[0] = kr2.reshape(tb, LANES).T[:ROPE_DIM]
    lane = lax.broadcasted_iota(jnp.int32, kr2.shape, 2)
    kcat_ref[:, :, 0:KV_LORA] = ckv.reshape(1, tb, KV_LORA).astype(BF16)
    kcat_ref[:, :, KV_LORA:QK_DIM] = jnp.where(lane < ROPE_DIM, kr2, 0.0).astype(BF16)
    ckvT = ckv.T.astype(BF16)
    csT = csT_ref[...]
    scale = ATTN_SCALE * LOG2_E
    for a in range(tb // tk):
        kT_ref[0, a] = ckvT[:, a * tk:(a + 1) * tk]
    for h in range(N_HEADS):
        q_latT = _dot_nt(wukT_ref[h], q_nope[:, h * NOPE_DIM:(h + 1) * NOPE_DIM].astype(BF16))
        t = _dot_nt(wqpT_ref[h], qn) * csT
        q_ropeT = t + jnp.concatenate([t[LANES // 2:], t[:LANES // 2]], axis=0)
        for a in range(tb // tq):
            cols = slice(a * tq, (a + 1) * tq)
            qT_ref[0, a, 0:KV_LORA, h * tq:(h + 1) * tq] = (q_latT[:, cols] * scale).astype(BF16)
            qT_ref[0, a, KV_LORA:QK_DIM, h * tq:(h + 1) * tq] = (q_ropeT[:, cols] * scale).astype(BF16)


def _pre_common_specs(sb, tb, w_a, w_qn):
    tile = lambda w: pl.BlockSpec((sb, tb, w), lambda i, j: (i, j, 0))
    mod_spec = lambda chunk: pl.BlockSpec((sb, 1, D_MODEL), lambda i, j: (i, 0, chunk))
    in_specs = [tile(D_MODEL), mod_spec(1), mod_spec(0),
                _const_spec((1, D_MODEL)), _const_spec(w_a.shape), _const_spec((1, Q_LORA)),
                _const_spec(w_qn.shape), _const_spec((1, KV_LORA)),
                pl.BlockSpec((tb, LANES), lambda i, j: (j, 0))]
    return tile, in_specs


def _pre_sample(x, mod, g_attn, w_a, g_q, w_qn, g_kv, cs, w_qp, w_uk, *, sb):
    s_tot, tb, _ = x.shape
    tile, in_specs = _pre_common_specs(sb, tb, w_a, w_qn)
    return pl.pallas_call(
        _pre_sample_kernel,
        out_shape=[jax.ShapeDtypeStruct((N_HEADS, s_tot, tb, QK_DIM), F32),
                   jax.ShapeDtypeStruct((s_tot, tb, KV_LORA), F32),
                   jax.ShapeDtypeStruct((s_tot, tb, ROPE_DIM), F32)],
        grid=(s_tot // sb, 1),
        in_specs=in_specs + [_const_spec(w_qp.shape), _const_spec(w_uk.shape)],
        out_specs=[pl.BlockSpec((N_HEADS, sb, tb, QK_DIM), lambda i, j: (0, i, j, 0)),
                   tile(KV_LORA), tile(ROPE_DIM)],
        compiler_params=pltpu.CompilerParams(dimension_semantics=("arbitrary", "arbitrary"),
                                             vmem_limit_bytes=VMEM_LIMIT),
        name="pre_sample",
    )(x, mod, mod, g_attn, w_a, g_q, w_qn, g_kv, cs, w_qp, w_uk)


def _pre_prompt(x, mod, g_attn, w_a, g_q, w_qn, g_kv, cs, w_qpT, w_ukT, csT, *, tb, tq, tk):
    b, t_tot, _ = x.shape
    assert tb % tq == 0 and tb % tk == 0
    tile, in_specs = _pre_common_specs(1, tb, w_a, w_qn)
    return pl.pallas_call(
        functools.partial(_pre_prompt_kernel, tq=tq, tk=tk),
        out_shape=[jax.ShapeDtypeStruct((b, t_tot // tq, QK_DIM, N_HEADS * tq), BF16),
                   jax.ShapeDtypeStruct((b, t_tot, KV_LORA), F32),
                   jax.ShapeDtypeStruct((b, ROPE_DIM, t_tot), F32),
                   jax.ShapeDtypeStruct((b, t_tot, QK_DIM), BF16),
                   jax.ShapeDtypeStruct((b, t_tot // tk, KV_LORA, tk), BF16)],
        grid=(b, t_tot // tb),
        in_specs=in_specs + [_const_spec(w_qpT.shape), _const_spec(w_ukT.shape),
                             pl.BlockSpec((LANES, tb), lambda i, j: (0, j))],
        out_specs=[pl.BlockSpec((1, tb // tq, QK_DIM, N_HEADS * tq), lambda i, j: (i, j, 0, 0)),
                   tile(KV_LORA), pl.BlockSpec((1, ROPE_DIM, tb), lambda i, j: (i, 0, j)), tile(QK_DIM),
                   pl.BlockSpec((1, tb // tk, KV_LORA, tk), lambda i, j: (i, j, 0, 0))],
        compiler_params=pltpu.CompilerParams(dimension_semantics=("arbitrary", "arbitrary"),
                                             vmem_limit_bytes=VMEM_LIMIT),
        name="pre_prompt",
    )(x, mod, mod, g_attn, w_a, g_q, w_qn, g_kv, cs, w_qpT, w_ukT, csT)


BLOCK_LOOKAHEAD = 3


def _attn_prompt_kernel(qT_ref, k_ref, kT_ref, wuvT_ref, o_ref, m_ref, l_ref, acc_ref, ahead_ref, *,
                        tq, tk, lookahead):
    i = pl.program_id(1)
    per = tq // tk
    m_ref[...] = jnp.full(m_ref.shape, NEG_BIG, F32)
    l_ref[...] = jnp.zeros(l_ref.shape, F32)
    acc_ref[...] = jnp.zeros(acc_ref.shape, F32)
    k_tok = lax.broadcasted_iota(jnp.int32, (tk, tk), 0)
    q_tok = lax.broadcasted_iota(jnp.int32, (tk, tk), 1)
    all_blocks = [(h, e) for h in range(N_HEADS) for e in range(per)]

    def cols(block):
        h, e = block
        return slice(h * tq + e * tk, h * tq + (e + 1) * tk)

    def scores(j, block):
        keys = k_ref[0, pl.ds(pl.multiple_of(j * tk, tk), tk), :]
        return _dot(keys, qT_ref[0, 0, :, cols(block)])

    def run(steps, after):
        pending = [ahead_ref[n] for n in range(lookahead)]
        for n, (j, block, on_diagonal) in enumerate(steps):
            ahead = n + lookahead
            if ahead < len(steps):
                pending.append(scores(*steps[ahead][:2]))
            elif ahead - len(steps) < len(after):
                ahead_ref[ahead - len(steps)] = scores(*after[ahead - len(steps)])
            sT = pending.pop(0)
            if on_diagonal:
                sT = jnp.where(k_tok <= q_tok, sT, NEG_BIG)
            c = cols(block)
            m_prev = m_ref[:, c]
            m_new = jnp.maximum(m_prev, jnp.max(sT, axis=0, keepdims=True))
            alpha = jnp.exp2(m_prev - m_new)
            pT = jnp.exp2(sT - m_new)
            l_ref[:, c] = alpha * l_ref[:, c] + jnp.sum(pT, axis=0, keepdims=True)
            acc_ref[:, c] = alpha * acc_ref[:, c] + _dot(kT_ref[0, j], pT.astype(BF16))
            m_ref[:, c] = m_new

    for n in range(lookahead):
        ahead_ref[n] = scores(0, all_blocks[n])

    def full_chunks(jj, carry):
        first = jj * per
        run([(first + c, block, False) for c in range(per) for block in all_blocks],
            [(first + per, block) for block in all_blocks[:lookahead]])
        return carry

    lax.fori_loop(0, i, full_chunks, 0)

    run([(i * per + c, (h, e), e == c) for c in range(per) for (h, e) in all_blocks if e >= c], [])

    for h in range(N_HEADS):
        head = slice(h * tq, (h + 1) * tq)
        oT = (acc_ref[:, head] * (1.0 / l_ref[:, head])).astype(BF16)
        o_ref[0, :, h * V_DIM:(h + 1) * V_DIM] = _dot(wuvT_ref[h], oT).T


def _attn_prompt(qT, kcat, kT, w_uvT, *, tq, tk):
    b, n_q, _, rows = qT.shape
    assert rows == N_HEADS * tq and tq % tk == 0 and BLOCK_LOOKAHEAD <= tq // tk * N_HEADS
    t = n_q * tq
    return pl.pallas_call(
        functools.partial(_attn_prompt_kernel, tq=tq, tk=tk, lookahead=BLOCK_LOOKAHEAD),
        out_shape=jax.ShapeDtypeStruct((b, t, N_HEADS * V_DIM), F32),
        grid=(b, n_q),
        in_specs=[pl.BlockSpec((1, 1, QK_DIM, rows), lambda bi, i: (bi, i, 0, 0)),
                  pl.BlockSpec((1, t, QK_DIM), lambda bi, i: (bi, 0, 0)),
                  pl.BlockSpec((1, t // tk, KV_LORA, tk), lambda bi, i: (bi, 0, 0, 0)),
                  _const_spec(w_uvT.shape)],
        out_specs=pl.BlockSpec((1, tq, N_HEADS * V_DIM), lambda bi, i: (bi, i, 0)),
        scratch_shapes=[pltpu.VMEM((1, rows), F32), pltpu.VMEM((1, rows), F32),
                        pltpu.VMEM((KV_LORA, rows), F32), pltpu.VMEM((BLOCK_LOOKAHEAD, tk, tk), F32)],
        compiler_params=pltpu.CompilerParams(dimension_semantics=("arbitrary", "arbitrary"),
                                             vmem_limit_bytes=VMEM_LIMIT),
        name="attn_prompt",
    )(qT, kcat, kT, w_uvT)


DMA_UNROLL = 8
SAMPLE_KEY_BLOCKS = 16
SAMPLE_LOOKAHEAD = 3
SEQS_PER_STEP = 4
FETCH_AHEAD = 2


def _attn_sample_kernel(pt_ref, q_ref, nk_ref, nr_ref, wuv_ref, ckv_hbm, krT_hbm, o_ref,
                        kbuf, rbuf, nkbuf, nrbuf, sem, *, n_pages, dec_t, n_blocks, lookahead):
    g = pl.program_id(0)
    n_steps = pl.num_programs(0)
    n_seq = n_steps * SEQS_PER_STEP
    rows = N_HEADS * dec_t
    past = n_pages * PAGE_SIZE

    def page_copies(page, p, slot):
        pos = pl.ds(pl.multiple_of(p * PAGE_SIZE, PAGE_SIZE), PAGE_SIZE)
        return (pltpu.make_async_copy(ckv_hbm.at[page], kbuf.at[slot, pos], sem.at[0, slot]),
                pltpu.make_async_copy(krT_hbm.at[page], rbuf.at[slot, :, pos], sem.at[1, slot]))

    def start_fetch(seq, slot):
        for p in range(n_pages):
            for cp in page_copies(pt_ref[seq, p], p, slot):
                cp.start()

    def wait_fetch(slot):
        def body(p, carry):
            for cp in page_copies(0, p, slot):
                cp.wait()
            return carry
        lax.fori_loop(0, n_pages, body, 0, unroll=DMA_UNROLL)

    def partial_softmax(s, k_lat):
        m_b = jnp.max(s, axis=-1, keepdims=True)
        p = jnp.exp(s - m_b)
        return m_b, jnp.sum(p, axis=-1, keepdims=True), _dot(p.astype(BF16), k_lat)

    def attend(slot):
        q = q_ref[:, slot].reshape(rows, QK_DIM)
        q_lat = q[:, :KV_LORA].astype(BF16)
        q_rope = q[:, KV_LORA:KV_LORA + ROPE_DIM].astype(BF16)

        def block_scores(b):
            ks = slice(b * past // n_blocks, (b + 1) * past // n_blocks)
            k_lat = kbuf[slot, ks, :].astype(BF16)
            return _dot_nt(q_lat, k_lat) + _dot(q_rope, rbuf[slot, :, ks].astype(BF16)), k_lat

        pending = [block_scores(b) for b in range(lookahead)]
        parts = []
        for b in range(n_blocks):
            if b + lookahead < n_blocks:
                pending.append(block_scores(b + lookahead))
            parts.append(partial_softmax(*pending.pop(0)))

        nkbuf[slot, 0:dec_t, :] = nk_ref[slot]
        nrbuf[slot, 0:dec_t, :] = nr_ref[slot]
        n_lat = nkbuf[slot].astype(BF16)
        s_new = _dot_nt(q_lat, n_lat) + _dot_nt(q_rope, nrbuf[slot].astype(BF16))
        q_tok = lax.broadcasted_iota(jnp.int32, (N_HEADS, dec_t, PAGE_SIZE), 1).reshape(rows, PAGE_SIZE)
        k_tok = lax.broadcasted_iota(jnp.int32, (rows, PAGE_SIZE), 1)
        s_new = jnp.where(k_tok <= q_tok, s_new, NEG_BIG)
        parts.append(partial_softmax(s_new, n_lat))

        m = functools.reduce(jnp.maximum, [m_b for m_b, _, _ in parts])
        weights = [jnp.exp(m_b - m) for m_b, _, _ in parts]
        l = functools.reduce(jnp.add, [w_b * l_b for w_b, (_, l_b, _) in zip(weights, parts)])
        acc = functools.reduce(jnp.add, [w_b * acc_b for w_b, (_, _, acc_b) in zip(weights, parts)])

        o = (acc / l).astype(BF16)
        for h in range(N_HEADS):
            o_ref[slot, :, h * V_DIM:(h + 1) * V_DIM] = _dot(o[h * dec_t:(h + 1) * dec_t], wuv_ref[h])

    first = g * SEQS_PER_STEP

    @pl.when(g == 0)
    def _():
        for a in range(FETCH_AHEAD):
            start_fetch(a, a)
        nkbuf[...] = jnp.zeros(nkbuf.shape, F32)
        nrbuf[...] = jnp.zeros(nrbuf.shape, F32)

    for a in range(SEQS_PER_STEP):
        wait_fetch(a)
        start_fetch(jnp.minimum(first + a + FETCH_AHEAD, n_seq - 1), (a + FETCH_AHEAD) % SEQS_PER_STEP)
        attend(a)

    @pl.when(g == n_steps - 1)
    def _():
        for a in range(FETCH_AHEAD):
            wait_fetch(a)


def _attn_sample(page_table, qcat, new_ckv, new_kr, w_uv, cache_ckv, cache_krT):
    _, n_seq, dec_t, _ = qcat.shape
    n_pages = page_table.shape[1]
    past = n_pages * PAGE_SIZE
    per = SEQS_PER_STEP
    assert n_seq % per == 0 and 0 < FETCH_AHEAD < per
    grid_spec = pltpu.PrefetchScalarGridSpec(
        num_scalar_prefetch=1,
        grid=(n_seq // per,),
        in_specs=[pl.BlockSpec((N_HEADS, per, dec_t, QK_DIM), lambda g, pt: (0, g, 0, 0)),
                  pl.BlockSpec((per, dec_t, KV_LORA), lambda g, pt: (g, 0, 0)),
                  pl.BlockSpec((per, dec_t, ROPE_DIM), lambda g, pt: (g, 0, 0)),
                  pl.BlockSpec(w_uv.shape, lambda g, pt: (0, 0, 0), pipeline_mode=pl.Buffered(1)),
                  pl.BlockSpec(memory_space=pl.ANY),
                  pl.BlockSpec(memory_space=pl.ANY)],
        out_specs=pl.BlockSpec((per, dec_t, N_HEADS * V_DIM), lambda g, pt: (g, 0, 0)),
        scratch_shapes=[pltpu.VMEM((per, past, KV_LORA), F32),
                        pltpu.VMEM((per, ROPE_DIM, past), F32),
                        pltpu.VMEM((per, PAGE_SIZE, KV_LORA), F32),
                        pltpu.VMEM((per, PAGE_SIZE, ROPE_DIM), F32),
                        pltpu.SemaphoreType.DMA((2, per))],
    )
    return pl.pallas_call(
        functools.partial(_attn_sample_kernel, n_pages=n_pages, dec_t=dec_t, n_blocks=SAMPLE_KEY_BLOCKS,
                          lookahead=SAMPLE_LOOKAHEAD),
        out_shape=jax.ShapeDtypeStruct((n_seq, dec_t, N_HEADS * V_DIM), F32),
        grid_spec=grid_spec,
        compiler_params=pltpu.CompilerParams(dimension_semantics=("arbitrary",),
                                             vmem_limit_bytes=VMEM_LIMIT),
        name="attn_sample",
    )(page_table, qcat, new_ckv, new_kr, w_uv, cache_ckv, cache_krT)


def _post1_kernel(x_ref, attn_ref, sc_ref, sh_ref, g1_ref, gattn_ref, wbc_ref, cw_ref, wo_ref,
                  prev_ref, x1_ref, nconv_ref, carry_ref, *, sb, tb, n_split):
    ts = tb // n_split
    m = sb * ts

    @pl.when(pl.program_id(1) == 0)
    def _():
        carry_ref[...] = prev_ref[...]

    def project(rows):
        x = x_ref[:, rows, :]
        u = _rms(x) * gattn_ref[...] * (1.0 + sc_ref[...]) + sh_ref[...]
        u = u.reshape(m, D_MODEL).astype(BF16)
        return x, _dot(u, wbc_ref[:, :3 * D_CONV]), _dot(u, wbc_ref[:, 3 * D_CONV:])

    def finish(rows, x, pb, pc):
        v = (pb[:, D_CONV:2 * D_CONV] * pb[:, 2 * D_CONV:]).reshape(sb, ts, D_CONV)
        p0 = carry_ref[:, 0:1, :]
        p1 = carry_ref[:, 1:2, :]
        t = lax.broadcasted_iota(jnp.int32, v.shape, 1)
        r1 = pltpu.roll(v, 1, axis=1)
        r2 = pltpu.roll(v, 2, axis=1)
        v1 = jnp.where(t >= 1, r1, p1)
        v2 = jnp.where(t >= 2, r2, jnp.where(t == 1, p1, p0))
        cw = cw_ref[...]
        z = cw[0:1, :] * v2 + cw[1:2, :] * v1 + cw[2:3, :] * v
        conv_out = pb[:, :D_CONV] * z.reshape(m, D_CONV)
        carry_ref[...] = r2[:, 0:CONV_W - 1, :]

        g = jax.nn.sigmoid(pc)
        merged = g[:, :D_MODEL] * attn_ref[:, rows, :].reshape(m, D_MODEL) + g[:, D_MODEL:] * conv_out
        proj = _dot(merged.astype(BF16), wo_ref[...]).reshape(sb, ts, D_MODEL)
        x1_ref[:, rows, :] = x + g1_ref[...] * proj

    ranges = [slice(r * ts, (r + 1) * ts) for r in range(n_split)]
    projected = [project(rows) for rows in ranges]
    for rows, args in zip(ranges, projected):
        finish(rows, *args)
    nconv_ref[...] = carry_ref[...]


def _post1(x, attn, mod, g_attn, w_bc, conv_w, w_o, prev, *, sb, tb, n_split):
    s_tot, t_tot, _ = x.shape
    tile = pl.BlockSpec((sb, tb, D_MODEL), lambda i, j: (i, j, 0))
    mod_spec = lambda chunk: pl.BlockSpec((sb, 1, D_MODEL), lambda i, j: (i, 0, chunk))
    state = pl.BlockSpec((sb, CONV_W - 1, D_CONV), lambda i, j: (i, 0, 0))
    return pl.pallas_call(
        functools.partial(_post1_kernel, sb=sb, tb=tb, n_split=n_split),
        out_shape=[jax.ShapeDtypeStruct(x.shape, F32),
                   jax.ShapeDtypeStruct((s_tot, CONV_W - 1, D_CONV), F32)],
        grid=(s_tot // sb, t_tot // tb),
        in_specs=[tile, tile, mod_spec(1), mod_spec(0), mod_spec(2),
                  _const_spec((1, D_MODEL)), _const_spec(w_bc.shape),
                  _const_spec(conv_w.shape), _const_spec(w_o.shape), state],
        out_specs=[tile, state],
        scratch_shapes=[pltpu.VMEM((sb, CONV_W - 1, D_CONV), F32)],
        compiler_params=pltpu.CompilerParams(dimension_semantics=("arbitrary", "arbitrary"),
                                             vmem_limit_bytes=VMEM_LIMIT),
        name="post1",
    )(x, attn, mod, mod, mod, g_attn, w_bc, conv_w, w_o, prev)


def _post2_kernel(x1_ref, sc_ref, sh_ref, g2_ref, gmlp_ref, w1_ref, w2_ref, *rest, sb, tb, ff_chunk):
    y_ref = rest[-1]
    m = sb * tb
    u = _modulated(x1_ref, sc_ref, sh_ref, gmlp_ref)
    acc = jnp.zeros((m, D_MODEL), F32)
    for c in range(D_FF // ff_chunk):
        cols = slice(c * ff_chunk, (c + 1) * ff_chunk)
        hdn = jnp.square(jnp.maximum(_dot(u, w1_ref[:, cols]), 0.0))
        acc = acc + _dot(hdn.astype(BF16), w2_ref[cols, :])
    x2 = x1_ref[...] + g2_ref[...] * acc.reshape(sb, tb, D_MODEL)
    y_ref[...] = _rms(x2) * rest[0][...] if len(rest) == 2 else x2


def _post2(x1, mod, g_mlp, w_1, w_2, maybe_g_final, *, sb, tb, ff_chunk):
    s_tot, t_tot, _ = x1.shape
    tile = pl.BlockSpec((sb, tb, D_MODEL), lambda i, j: (i, j, 0))
    mod_spec = lambda chunk: pl.BlockSpec((sb, 1, D_MODEL), lambda i, j: (i, 0, chunk))
    return pl.pallas_call(
        functools.partial(_post2_kernel, sb=sb, tb=tb, ff_chunk=ff_chunk),
        out_shape=jax.ShapeDtypeStruct(x1.shape, F32),
        grid=(s_tot // sb, t_tot // tb),
        in_specs=[tile, mod_spec(4), mod_spec(3), mod_spec(5),
                  _const_spec((1, D_MODEL)), _const_spec(w_1.shape), _const_spec(w_2.shape)]
                 + [_const_spec((1, D_MODEL)) for _ in maybe_g_final],
        out_specs=tile,
        compiler_params=pltpu.CompilerParams(dimension_semantics=("arbitrary", "arbitrary"),
                                             vmem_limit_bytes=VMEM_LIMIT),
        name="post2",
    )(x1, mod, mod, mod, g_mlp, w_1, w_2, *maybe_g_final)


def _rope_table(pos):
    inv = ROPE_THETA ** (-jnp.arange(0, ROPE_DIM, 2, dtype=F32) / ROPE_DIM)
    ang = pos.astype(F32)[:, None] * inv[None, :]
    cos, sin = jnp.cos(ang), jnp.sin(ang)
    return jnp.concatenate([cos, cos, -sin, sin], axis=-1)


def _swap_halves(w):
    half = w.shape[-1] // 2
    return jnp.concatenate([w[..., half:], w[..., :half]], axis=-1)


def _layer_weights(w_in, w_q_b, w_kv_b, w_o, w_1, w_2):
    s1, s2 = Q_LORA + KV_LORA, Q_LORA + KV_LORA + ROPE_DIM
    w_kr = w_in[:, s1:s2]
    wq = w_q_b.reshape(Q_LORA, N_HEADS, NOPE_DIM + ROPE_DIM)
    wq_rope = wq[:, :, NOPE_DIM:]
    wq_pair = jnp.concatenate([wq_rope, _swap_halves(wq_rope)], axis=-1)
    wkv = w_kv_b.reshape(KV_LORA, N_HEADS, NOPE_DIM + V_DIM)
    w_uk, w_uv = wkv[:, :, :NOPE_DIM], wkv[:, :, NOPE_DIM:]
    weights = dict(
        w_a=jnp.concatenate([w_in[:, :s1], w_kr, _swap_halves(w_kr)], axis=1),
        w_bc=w_in[:, s2:],
        w_qn=wq[:, :, :NOPE_DIM].reshape(Q_LORA, N_HEADS * NOPE_DIM),
        w_qp=wq_pair.reshape(Q_LORA, N_HEADS * LANES),
        w_qpT=jnp.transpose(wq_pair, (1, 2, 0)),
        w_uk=jnp.transpose(w_uk, (1, 2, 0)),
        w_ukT=jnp.transpose(w_uk, (1, 0, 2)),
        w_uv=jnp.transpose(w_uv, (1, 0, 2)),
        w_uvT=jnp.transpose(w_uv, (1, 2, 0)),
        w_o=w_o, w_1=w_1, w_2=w_2)
    return {k: v.astype(BF16) for k, v in weights.items()}


PROMPT_TILE = 512
PROMPT_SPLIT = 2
QUERY_TILE = 512
KEY_CHUNK = 256
SAMPLE_SEQS = 64
FF_CHUNK = 1024


def kernel(x_prompt, x_sample, cache_ckv, cache_krope, state_conv, page_table, c_prompt, c_sample,
           w_ada, b_ada, g_attn, w_in, g_q, w_q_b, g_kv, w_kv_b, conv_w, w_o, g_mlp, w_1, w_2, g_final):
    depth = w_in.shape[0]
    batch, seq, _ = x_prompt.shape
    dec_b, dec_t, _ = x_sample.shape
    past_len = page_table.shape[1] * PAGE_SIZE
    cs_p = _rope_table(jnp.arange(seq, dtype=jnp.int32))
    cs_s = _rope_table(past_len + jnp.arange(dec_t, dtype=jnp.int32))
    n_c = batch + dec_b
    c_all = jnp.concatenate([c_prompt, c_sample, jnp.zeros((-n_c % 8, D_MODEL), F32)], axis=0)
    row = lambda g: g.reshape(1, -1)

    hp, hs = x_prompt, x_sample
    outs = [[] for _ in range(6)]
    for l in range(depth):
        w = _layer_weights(w_in[l], w_q_b[l], w_kv_b[l], w_o[l], w_1[l], w_2[l])
        mod = _ada(c_all, w_ada[l], row(b_ada[l]))
        mod_p = mod[:batch].reshape(batch, 1, -1)
        mod_s = mod[batch:n_c].reshape(dec_b, 1, -1)
        shared = (row(g_attn[l]), w["w_a"], row(g_q[l]), w["w_qn"], row(g_kv[l]))

        qT, ckv_p, krT_p, kcat, kT = _pre_prompt(hp, mod_p, *shared, cs_p, w["w_qpT"], w["w_ukT"], cs_p.T,
                                                 tb=PROMPT_TILE, tq=QUERY_TILE, tk=KEY_CHUNK)
        kr_p = jnp.swapaxes(krT_p, 1, 2)
        attn_p = _attn_prompt(qT, kcat, kT, w["w_uvT"], tq=QUERY_TILE, tk=KEY_CHUNK)
        buf0 = jnp.zeros((batch, CONV_W - 1, D_CONV), F32)
        x1_p, conv_p = _post1(hp, attn_p, mod_p, row(g_attn[l]), w["w_bc"], conv_w[l], w["w_o"], buf0,
                              sb=1, tb=PROMPT_TILE, n_split=PROMPT_SPLIT)

        qcat_s, ckv_s, kr_s = _pre_sample(hs, mod_s, *shared, cs_s, w["w_qp"], w["w_uk"], sb=SAMPLE_SEQS)
        attn_s = _attn_sample(page_table, qcat_s, ckv_s, kr_s, w["w_uv"], cache_ckv[l],
                              jnp.swapaxes(cache_krope[l], 1, 2))
        x1_s, conv_s = _post1(hs, attn_s, mod_s, row(g_attn[l]), w["w_bc"], conv_w[l], w["w_o"],
                              state_conv[l], sb=SAMPLE_SEQS, tb=dec_t, n_split=1)

        g_fin = (row(g_final),) if l == depth - 1 else ()
        hp = _post2(x1_p, mod_p, row(g_mlp[l]), w["w_1"], w["w_2"], g_fin, sb=1, tb=PROMPT_TILE,
                    ff_chunk=FF_CHUNK)
        hs = _post2(x1_s, mod_s, row(g_mlp[l]), w["w_1"], w["w_2"], g_fin, sb=SAMPLE_SEQS, tb=dec_t,
                    ff_chunk=FF_CHUNK)
        for acc, val in zip(outs, (ckv_p, kr_p, conv_p, ckv_s, kr_s, conv_s)):
            acc.append(val)
    return (hp, hs) + tuple(jnp.stack(o) for o in outs)
```

```python
import functools

import jax
import jax.numpy as jnp
from jax import lax
from jax.experimental import pallas as pl
from jax.experimental.pallas import tpu as pltpu

D_MODEL = 1024
N_HEADS = 8
Q_LORA = 384
KV_LORA = 256
NOPE_DIM = 128
ROPE_DIM = 64
V_DIM = D_MODEL // N_HEADS
D_CONV = D_MODEL
CONV_W = 3
D_FF = 4 * D_MODEL
PAGE_SIZE = 128
ROPE_THETA = 10000.0
EPS = 1e-6
ATTN_SCALE = (NOPE_DIM + ROPE_DIM) ** -0.5
LOG2_E = 1.4426950408889634

LANES = 128
QK_DIM = KV_LORA + LANES
NEG_BIG = -1e30
VMEM_LIMIT = 56 * 1024 * 1024

F32 = jnp.float32
BF16 = jnp.bfloat16


def _const_spec(shape):
    zeros = (0,) * len(shape)
    return pl.BlockSpec(shape, lambda *_: zeros, pipeline_mode=pl.Buffered(1))


def _dot(a, b):
    return jnp.dot(a, b, preferred_element_type=F32)


def _dot_nt(a, b):
    return lax.dot_general(a, b, (((1,), (1,)), ((), ())), preferred_element_type=F32)


def _rms(x):
    return x * lax.rsqrt(jnp.mean(x * x, axis=-1, keepdims=True) + EPS)


def _ada_kernel(c_ref, w_ref, b_ref, o_ref):
    c = c_ref[...]
    a = (c * jax.nn.sigmoid(c)).astype(BF16)
    o_ref[...] = _dot(a, w_ref[...].astype(BF16)) + b_ref[...]


def _ada(c_all, w_ada, b_ada):
    rows = c_all.shape[0]
    n_out = w_ada.shape[1]
    bn = D_MODEL
    return pl.pallas_call(
        _ada_kernel,
        out_shape=jax.ShapeDtypeStruct((rows, n_out), F32),
        grid=(n_out // bn,),
        in_specs=[pl.BlockSpec((rows, D_MODEL), lambda j: (0, 0)),
                  pl.BlockSpec((D_MODEL, bn), lambda j: (0, j)),
                  pl.BlockSpec((1, bn), lambda j: (0, j))],
        out_specs=pl.BlockSpec((rows, bn), lambda j: (0, j)),
        compiler_params=pltpu.CompilerParams(dimension_semantics=("arbitrary",),
                                             vmem_limit_bytes=VMEM_LIMIT),
        name="ada",
    )(c_all, w_ada, b_ada)


def _modulated(x_ref, sc_ref, sh_ref, g_ref):
    u = _rms(x_ref[...]) * g_ref[...] * (1.0 + sc_ref[...]) + sh_ref[...]
    sb, tb, d = u.shape
    return u.reshape(sb * tb, d).astype(BF16)


def _rope_pair(chunk, cs):
    t = chunk * cs
    return t + pltpu.roll(t, LANES // 2, axis=2)


def _latent_and_query(x_ref, sc_ref, sh_ref, gattn_ref, wa_ref, gq_ref, wqn_ref, gkv_ref, cs_ref, ckv_ref):
    sb, tb, _ = x_ref.shape
    u = _modulated(x_ref, sc_ref, sh_ref, gattn_ref)
    pa = _dot(u, wa_ref[...])
    ckv = _rms(pa[:, Q_LORA:Q_LORA + KV_LORA]) * gkv_ref[...]
    ckv_ref[...] = ckv.reshape(sb, tb, KV_LORA)
    kr2 = _rope_pair(pa[:, Q_LORA + KV_LORA:].reshape(sb, tb, LANES), cs_ref[...])
    qn = (_rms(pa[:, :Q_LORA]) * gq_ref[...]).astype(BF16)
    q_nope = _dot(qn, wqn_ref[...])
    return ckv, kr2, qn, q_nope


def _pre_sample_kernel(x_ref, sc_ref, sh_ref, gattn_ref, wa_ref, gq_ref, wqn_ref, gkv_ref, cs_ref,
                       wqp_ref, wuk_ref, qcat_ref, ckv_ref, kr_ref):
    sb, tb, _ = x_ref.shape
    _, kr2, qn, q_nope = _latent_and_query(x_ref, sc_ref, sh_ref, gattn_ref, wa_ref, gq_ref, wqn_ref,
                                           gkv_ref, cs_ref, ckv_ref)
    kr_ref[...] = kr2[:, :, :ROPE_DIM]
    q_pair = _dot(qn, wqp_ref[...])
    for h in range(N_HEADS):
        q_lat = _dot(q_nope[:, h * NOPE_DIM:(h + 1) * NOPE_DIM].astype(BF16), wuk_ref[h]) * ATTN_SCALE
        q_rope = _rope_pair(q_pair[:, h * LANES:(h + 1) * LANES].reshape(sb, tb, LANES), cs_ref[...])
        qcat_ref[h, :, :, 0:KV_LORA] = q_lat.reshape(sb, tb, KV_LORA)
        qcat_ref[h, :, :, KV_LORA:QK_DIM] = q_rope * ATTN_SCALE


def _pre_prompt_kernel(x_ref, sc_ref, sh_ref, gattn_ref, wa_ref, gq_ref, wqn_ref, gkv_ref, cs_ref,
                       wqpT_ref, wukT_ref, csT_ref, qT_ref, ckv_ref, krT_ref, kcat_ref, kT_ref, *, tq, tk):
    _, tb, _ = x_ref.shape
    ckv, kr2, qn, q_nope = _latent_and_query(x_ref, sc_ref, sh_ref, gattn_ref, wa_ref, gq_ref, wqn_ref,
                                             gkv_ref, cs_ref, ckv_ref)
    krT_ref[0] = kr2.reshape(tb, LANES).T[:ROPE_DIM]
    lane = lax.broadcasted_iota(jnp.int32, kr2.shape, 2)
    kcat_ref[:, :, 0:KV_LORA] = ckv.reshape(1, tb, KV_LORA).astype(BF16)
    kcat_ref[:, :, KV_LORA:QK_DIM] = jnp.where(lane < ROPE_DIM, kr2, 0.0).astype(BF16)
    ckvT = ckv.T.astype(BF16)
    csT = csT_ref[...]
    scale = ATTN_SCALE * LOG2_E
    for a in range(tb // tk):
        kT_ref[0, a] = ckvT[:, a * tk:(a + 1) * tk]
    for h in range(N_HEADS):
        q_latT = _dot_nt(wukT_ref[h], q_nope[:, h * NOPE_DIM:(h + 1) * NOPE_DIM].astype(BF16))
        t = _dot_nt(wqpT_ref[h], qn) * csT
        q_ropeT = t + jnp.concatenate([t[LANES // 2:], t[:LANES // 2]], axis=0)
        for a in range(tb // tq):
            cols = slice(a * tq, (a + 1) * tq)
            qT_ref[0, a, 0:KV_LORA, h * tq:(h + 1) * tq] = (q_latT[:, cols] * scale).astype(BF16)
            qT_ref[0, a, KV_LORA:QK_DIM, h * tq:(h + 1) * tq] = (q_ropeT[:, cols] * scale).astype(BF16)


def _pre_common_specs(sb, tb, w_a, w_qn):
    tile = lambda w: pl.BlockSpec((sb, tb, w), lambda i, j: (i, j, 0))
    mod_spec = lambda chunk: pl.BlockSpec((sb, 1, D_MODEL), lambda i, j: (i, 0, chunk))
    in_specs = [tile(D_MODEL), mod_spec(1), mod_spec(0),
                _const_spec((1, D_MODEL)), _const_spec(w_a.shape), _const_spec((1, Q_LORA)),
                _const_spec(w_qn.shape), _const_spec((1, KV_LORA)),
                pl.BlockSpec((tb, LANES), lambda i, j: (j, 0))]
    return tile, in_specs


def _pre_sample(x, mod, g_attn, w_a, g_q, w_qn, g_kv, cs, w_qp, w_uk, *, sb):
    s_tot, tb, _ = x.shape
    tile, in_specs = _pre_common_specs(sb, tb, w_a, w_qn)
    return pl.pallas_call(
        _pre_sample_kernel,
        out_shape=[jax.ShapeDtypeStruct((N_HEADS, s_tot, tb, QK_DIM), F32),
                   jax.ShapeDtypeStruct((s_tot, tb, KV_LORA), F32),
                   jax.ShapeDtypeStruct((s_tot, tb, ROPE_DIM), F32)],
        grid=(s_tot // sb, 1),
        in_specs=in_specs + [_const_spec(w_qp.shape), _const_spec(w_uk.shape)],
        out_specs=[pl.BlockSpec((N_HEADS, sb, tb, QK_DIM), lambda i, j: (0, i, j, 0)),
                   tile(KV_LORA), tile(ROPE_DIM)],
        compiler_params=pltpu.CompilerParams(dimension_semantics=("arbitrary", "arbitrary"),
                                             vmem_limit_bytes=VMEM_LIMIT),
        name="pre_sample",
    )(x, mod, mod, g_attn, w_a, g_q, w_qn, g_kv, cs, w_qp, w_uk)


def _pre_prompt(x, mod, g_attn, w_a, g_q, w_qn, g_kv, cs, w_qpT, w_ukT, csT, *, tb, tq, tk):
    b, t_tot, _ = x.shape
    assert tb % tq == 0 and tb % tk == 0
    tile, in_specs = _pre_common_specs(1, tb, w_a, w_qn)
    return pl.pallas_call(
        functools.partial(_pre_prompt_kernel, tq=tq, tk=tk),
        out_shape=[jax.ShapeDtypeStruct((b, t_tot // tq, QK_DIM, N_HEADS * tq), BF16),
                   jax.ShapeDtypeStruct((b, t_tot, KV_LORA), F32),
                   jax.ShapeDtypeStruct((b, ROPE_DIM, t_tot), F32),
                   jax.ShapeDtypeStruct((b, t_tot, QK_DIM), BF16),
                   jax.ShapeDtypeStruct((b, t_tot // tk, KV_LORA, tk), BF16)],
        grid=(b, t_tot // tb),
        in_specs=in_specs + [_const_spec(w_qpT.shape), _const_spec(w_ukT.shape),
                             pl.BlockSpec((LANES, tb), lambda i, j: (0, j))],
        out_specs=[pl.BlockSpec((1, tb // tq, QK_DIM, N_HEADS * tq), lambda i, j: (i, j, 0, 0)),
                   tile(KV_LORA), pl.BlockSpec((1, ROPE_DIM, tb), lambda i, j: (i, 0, j)), tile(QK_DIM),
                   pl.BlockSpec((1, tb // tk, KV_LORA, tk), lambda i, j: (i, j, 0, 0))],
        compiler_params=pltpu.CompilerParams(dimension_semantics=("arbitrary", "arbitrary"),
                                             vmem_limit_bytes=VMEM_LIMIT),
        name="pre_prompt",
    )(x, mod, mod, g_attn, w_a, g_q, w_qn, g_kv, cs, w_qpT, w_ukT, csT)


BLOCK_LOOKAHEAD = 3


def _attn_prompt_kernel(qT_ref, k_ref, kT_ref, wuvT_ref, o_ref, m_ref, l_ref, acc_ref, ahead_ref, *,
                        tq, tk, lookahead):
    i = pl.program_id(1)
    per = tq // tk
    m_ref[...] = jnp.full(m_ref.shape, NEG_BIG, F32)
    l_ref[...] = jnp.zeros(l_ref.shape, F32)
    acc_ref[...] = jnp.zeros(acc_ref.shape, F32)
    k_tok = lax.broadcasted_iota(jnp.int32, (tk, tk), 0)
    q_tok = lax.broadcasted_iota(jnp.int32, (tk, tk), 1)
    all_blocks = [(h, e) for h in range(N_HEADS) for e in range(per)]

    def cols(block):
        h, e = block
        return slice(h * tq + e * tk, h * tq + (e + 1) * tk)

    def scores(j, block):
        keys = k_ref[0, pl.ds(pl.multiple_of(j * tk, tk), tk), :]
        return _dot(keys, qT_ref[0, 0, :, cols(block)])

    def run(steps, after):
        pending = [ahead_ref[n] for n in range(lookahead)]
        for n, (j, block, on_diagonal) in enumerate(steps):
            ahead = n + lookahead
            if ahead < len(steps):
                pending.append(scores(*steps[ahead][:2]))
            elif ahead - len(steps) < len(after):
                ahead_ref[ahead - len(steps)] = scores(*after[ahead - len(steps)])
            sT = pending.pop(0)
            if on_diagonal:
                sT = jnp.where(k_tok <= q_tok, sT, NEG_BIG)
            c = cols(block)
            m_prev = m_ref[:, c]
            m_new = jnp.maximum(m_prev, jnp.max(sT, axis=0, keepdims=True))
            alpha = jnp.exp2(m_prev - m_new)
            pT = jnp.exp2(sT - m_new)
            l_ref[:, c] = alpha * l_ref[:, c] + jnp.sum(pT, axis=0, keepdims=True)
            acc_ref[:, c] = alpha * acc_ref[:, c] + _dot(kT_ref[0, j], pT.astype(BF16))
            m_ref[:, c] = m_new

    for n in range(lookahead):
        ahead_ref[n] = scores(0, all_blocks[n])

    def full_chunks(jj, carry):
        first = jj * per
        run([(first + c, block, False) for c in range(per) for block in all_blocks],
            [(first + per, block) for block in all_blocks[:lookahead]])
        return carry

    lax.fori_loop(0, i, full_chunks, 0)

    run([(i * per + c, (h, e), e == c) for c in range(per) for (h, e) in all_blocks if e >= c], [])

    for h in range(N_HEADS):
        head = slice(h * tq, (h + 1) * tq)
        oT = (acc_ref[:, head] * (1.0 / l_ref[:, head])).astype(BF16)
        o_ref[0, :, h * V_DIM:(h + 1) * V_DIM] = _dot(wuvT_ref[h], oT).T


def _attn_prompt(qT, kcat, kT, w_uvT, *, tq, tk):
    b, n_q, _, rows = qT.shape
    assert rows == N_HEADS * tq and tq % tk == 0 and BLOCK_LOOKAHEAD <= tq // tk * N_HEADS
    t = n_q * tq
    return pl.pallas_call(
        functools.partial(_attn_prompt_kernel, tq=tq, tk=tk, lookahead=BLOCK_LOOKAHEAD),
        out_shape=jax.ShapeDtypeStruct((b, t, N_HEADS * V_DIM), F32),
        grid=(b, n_q),
        in_specs=[pl.BlockSpec((1, 1, QK_DIM, rows), lambda bi, i: (bi, i, 0, 0)),
                  pl.BlockSpec((1, t, QK_DIM), lambda bi, i: (bi, 0, 0)),
                  pl.BlockSpec((1, t // tk, KV_LORA, tk), lambda bi, i: (bi, 0, 0, 0)),
                  _const_spec(w_uvT.shape)],
        out_specs=pl.BlockSpec((1, tq, N_HEADS * V_DIM), lambda bi, i: (bi, i, 0)),
        scratch_shapes=[pltpu.VMEM((1, rows), F32), pltpu.VMEM((1, rows), F32),
                        pltpu.VMEM((KV_LORA, rows), F32), pltpu.VMEM((BLOCK_LOOKAHEAD, tk, tk), F32)],
        compiler_params=pltpu.CompilerParams(dimension_semantics=("arbitrary", "arbitrary"),
                                             vmem_limit_bytes=VMEM_LIMIT),
        name="attn_prompt",
    )(qT, kcat, kT, w_uvT)


DMA_UNROLL = 8
SAMPLE_KEY_BLOCKS = 16
SAMPLE_LOOKAHEAD = 3
SEQS_PER_STEP = 4
FETCH_AHEAD = 3


def _attn_sample_kernel(pt_ref, q_ref, nk_ref, nr_ref, wuv_ref, ckv_hbm, krT_hbm, o_ref,
                        kbuf, rbuf, nkbuf, nrbuf, sem, *, n_pages, dec_t, n_blocks, lookahead):
    g = pl.program_id(0)
    n_steps = pl.num_programs(0)
    n_seq = n_steps * SEQS_PER_STEP
    rows = N_HEADS * dec_t
    past = n_pages * PAGE_SIZE

    def page_copies(page, p, slot):
        pos = pl.ds(pl.multiple_of(p * PAGE_SIZE, PAGE_SIZE), PAGE_SIZE)
        return (pltpu.make_async_copy(ckv_hbm.at[page], kbuf.at[slot, pos], sem.at[0, slot]),
                pltpu.make_async_copy(krT_hbm.at[page], rbuf.at[slot, :, pos], sem.at[1, slot]))

    def start_fetch(seq, slot):
        for p in range(n_pages):
            for cp in page_copies(pt_ref[seq, p], p, slot):
                cp.start()

    def wait_fetch(slot):
        def body(p, carry):
            for cp in page_copies(0, p, slot):
                cp.wait()
            return carry
        lax.fori_loop(0, n_pages, body, 0, unroll=DMA_UNROLL)

    def partial_softmax(s, k_lat):
        m_b = jnp.max(s, axis=-1, keepdims=True)
        p = jnp.exp(s - m_b)
        return m_b, jnp.sum(p, axis=-1, keepdims=True), _dot(p.astype(BF16), k_lat)

    def attend(slot):
        q = q_ref[:, slot].reshape(rows, QK_DIM)
        q_lat = q[:, :KV_LORA].astype(BF16)
        q_rope = q[:, KV_LORA:KV_LORA + ROPE_DIM].astype(BF16)

        def block_scores(b):
            ks = slice(b * past // n_blocks, (b + 1) * past // n_blocks)
            k_lat = kbuf[slot, ks, :].astype(BF16)
            return _dot_nt(q_lat, k_lat) + _dot(q_rope, rbuf[slot, :, ks].astype(BF16)), k_lat

        pending = [block_scores(b) for b in range(lookahead)]
        parts = []
        for b in range(n_blocks):
            if b + lookahead < n_blocks:
                pending.append(block_scores(b + lookahead))
            parts.append(partial_softmax(*pending.pop(0)))

        nkbuf[slot, 0:dec_t, :] = nk_ref[slot]
        nrbuf[slot, 0:dec_t, :] = nr_ref[slot]
        n_lat = nkbuf[slot].astype(BF16)
        s_new = _dot_nt(q_lat, n_lat) + _dot_nt(q_rope, nrbuf[slot].astype(BF16))
        q_tok = lax.broadcasted_iota(jnp.int32, (N_HEADS, dec_t, PAGE_SIZE), 1).reshape(rows, PAGE_SIZE)
        k_tok = lax.broadcasted_iota(jnp.int32, (rows, PAGE_SIZE), 1)
        s_new = jnp.where(k_tok <= q_tok, s_new, NEG_BIG)
        parts.append(partial_softmax(s_new, n_lat))

        m = functools.reduce(jnp.maximum, [m_b for m_b, _, _ in parts])
        weights = [jnp.exp(m_b - m) for m_b, _, _ in parts]
        l = functools.reduce(jnp.add, [w_b * l_b for w_b, (_, l_b, _) in zip(weights, parts)])
        acc = functools.reduce(jnp.add, [w_b * acc_b for w_b, (_, _, acc_b) in zip(weights, parts)])

        o = (acc / l).astype(BF16)
        for h in range(N_HEADS):
            o_ref[slot, :, h * V_DIM:(h + 1) * V_DIM] = _dot(o[h * dec_t:(h + 1) * dec_t], wuv_ref[h])

    first = g * SEQS_PER_STEP

    @pl.when(g == 0)
    def _():
        for a in range(FETCH_AHEAD):
            start_fetch(a, a)
        nkbuf[...] = jnp.zeros(nkbuf.shape, F32)
        nrbuf[...] = jnp.zeros(nrbuf.shape, F32)

    for a in range(SEQS_PER_STEP):
        wait_fetch(a)
        start_fetch(jnp.minimum(first + a + FETCH_AHEAD, n_seq - 1), (a + FETCH_AHEAD) % SEQS_PER_STEP)
        attend(a)

    @pl.when(g == n_steps - 1)
    def _():
        for a in range(FETCH_AHEAD):
            wait_fetch(a)


def _attn_sample(page_table, qcat, new_ckv, new_kr, w_uv, cache_ckv, cache_krT):
    _, n_seq, dec_t, _ = qcat.shape
    n_pages = page_table.shape[1]
    past = n_pages * PAGE_SIZE
    per = SEQS_PER_STEP
    assert n_seq % per == 0 and 0 < FETCH_AHEAD < per
    grid_spec = pltpu.PrefetchScalarGridSpec(
        num_scalar_prefetch=1,
        grid=(n_seq // per,),
        in_specs=[pl.BlockSpec((N_HEADS, per, dec_t, QK_DIM), lambda g, pt: (0, g, 0, 0)),
                  pl.BlockSpec((per, dec_t, KV_LORA), lambda g, pt: (g, 0, 0)),
                  pl.BlockSpec((per, dec_t, ROPE_DIM), lambda g, pt: (g, 0, 0)),
                  pl.BlockSpec(w_uv.shape, lambda g, pt: (0, 0, 0), pipeline_mode=pl.Buffered(1)),
                  pl.BlockSpec(memory_space=pl.ANY),
                  pl.BlockSpec(memory_space=pl.ANY)],
        out_specs=pl.BlockSpec((per, dec_t, N_HEADS * V_DIM), lambda g, pt: (g, 0, 0)),
        scratch_shapes=[pltpu.VMEM((per, past, KV_LORA), F32),
                        pltpu.VMEM((per, ROPE_DIM, past), F32),
                        pltpu.VMEM((per, PAGE_SIZE, KV_LORA), F32),
                        pltpu.VMEM((per, PAGE_SIZE, ROPE_DIM), F32),
                        pltpu.SemaphoreType.DMA((2, per))],
    )
    return pl.pallas_call(
        functools.partial(_attn_sample_kernel, n_pages=n_pages, dec_t=dec_t, n_blocks=SAMPLE_KEY_BLOCKS,
                          lookahead=SAMPLE_LOOKAHEAD),
        out_shape=jax.ShapeDtypeStruct((n_seq, dec_t, N_HEADS * V_DIM), F32),
        grid_spec=grid_spec,
        compiler_params=pltpu.CompilerParams(dimension_semantics=("arbitrary",),
                                             vmem_limit_bytes=VMEM_LIMIT),
        name="attn_sample",
    )(page_table, qcat, new_ckv, new_kr, w_uv, cache_ckv, cache_krT)


def _post1_kernel(x_ref, attn_ref, sc_ref, sh_ref, g1_ref, gattn_ref, wbc_ref, cw_ref, wo_ref,
                  prev_ref, x1_ref, nconv_ref, carry_ref, *, sb, tb, n_split):
    ts = tb // n_split
    m = sb * ts

    @pl.when(pl.program_id(1) == 0)
    def _():
        carry_ref[...] = prev_ref[...]

    def project(rows):
        x = x_ref[:, rows, :]
        u = _rms(x) * gattn_ref[...] * (1.0 + sc_ref[...]) + sh_ref[...]
        u = u.reshape(m, D_MODEL).astype(BF16)
        return x, _dot(u, wbc_ref[:, :3 * D_CONV]), _dot(u, wbc_ref[:, 3 * D_CONV:])

    def finish(rows, x, pb, pc):
        v = (pb[:, D_CONV:2 * D_CONV] * pb[:, 2 * D_CONV:]).reshape(sb, ts, D_CONV)
        p0 = carry_ref[:, 0:1, :]
        p1 = carry_ref[:, 1:2, :]
        t = lax.broadcasted_iota(jnp.int32, v.shape, 1)
        r1 = pltpu.roll(v, 1, axis=1)
        r2 = pltpu.roll(v, 2, axis=1)
        v1 = jnp.where(t >= 1, r1, p1)
        v2 = jnp.where(t >= 2, r2, jnp.where(t == 1, p1, p0))
        cw = cw_ref[...]
        z = cw[0:1, :] * v2 + cw[1:2, :] * v1 + cw[2:3, :] * v
        conv_out = pb[:, :D_CONV] * z.reshape(m, D_CONV)
        carry_ref[...] = r2[:, 0:CONV_W - 1, :]

        g = jax.nn.sigmoid(pc)
        merged = g[:, :D_MODEL] * attn_ref[:, rows, :].reshape(m, D_MODEL) + g[:, D_MODEL:] * conv_out
        proj = _dot(merged.astype(BF16), wo_ref[...]).reshape(sb, ts, D_MODEL)
        x1_ref[:, rows, :] = x + g1_ref[...] * proj

    ranges = [slice(r * ts, (r + 1) * ts) for r in range(n_split)]
    projected = [project(rows) for rows in ranges]
    for rows, args in zip(ranges, projected):
        finish(rows, *args)
    nconv_ref[...] = carry_ref[...]


def _post1(x, attn, mod, g_attn, w_bc, conv_w, w_o, prev, *, sb, tb, n_split):
    s_tot, t_tot, _ = x.shape
    tile = pl.BlockSpec((sb, tb, D_MODEL), lambda i, j: (i, j, 0))
    mod_spec = lambda chunk: pl.BlockSpec((sb, 1, D_MODEL), lambda i, j: (i, 0, chunk))
    state = pl.BlockSpec((sb, CONV_W - 1, D_CONV), lambda i, j: (i, 0, 0))
    return pl.pallas_call(
        functools.partial(_post1_kernel, sb=sb, tb=tb, n_split=n_split),
        out_shape=[jax.ShapeDtypeStruct(x.shape, F32),
                   jax.ShapeDtypeStruct((s_tot, CONV_W - 1, D_CONV), F32)],
        grid=(s_tot // sb, t_tot // tb),
        in_specs=[tile, tile, mod_spec(1), mod_spec(0), mod_spec(2),
                  _const_spec((1, D_MODEL)), _const_spec(w_bc.shape),
                  _const_spec(conv_w.shape), _const_spec(w_o.shape), state],
        out_specs=[tile, state],
        scratch_shapes=[pltpu.VMEM((sb, CONV_W - 1, D_CONV), F32)],
        compiler_params=pltpu.CompilerParams(dimension_semantics=("arbitrary", "arbitrary"),
                                             vmem_limit_bytes=VMEM_LIMIT),
        name="post1",
    )(x, attn, mod, mod, mod, g_attn, w_bc, conv_w, w_o, prev)


def _post2_kernel(x1_ref, sc_ref, sh_ref, g2_ref, gmlp_ref, w1_ref, w2_ref, *rest, sb, tb, ff_chunk):
    y_ref = rest[-1]
    m = sb * tb
    u = _modulated(x1_ref, sc_ref, sh_ref, gmlp_ref)
    acc = jnp.zeros((m, D_MODEL), F32)
    for c in range(D_FF // ff_chunk):
        cols = slice(c * ff_chunk, (c + 1) * ff_chunk)
        hdn = jnp.square(jnp.maximum(_dot(u, w1_ref[:, cols]), 0.0))
        acc = acc + _dot(hdn.astype(BF16), w2_ref[cols, :])
    x2 = x1_ref[...] + g2_ref[...] * acc.reshape(sb, tb, D_MODEL)
    y_ref[...] = _rms(x2) * rest[0][...] if len(rest) == 2 else x2


def _post2(x1, mod, g_mlp, w_1, w_2, maybe_g_final, *, sb, tb, ff_chunk):
    s_tot, t_tot, _ = x1.shape
    tile = pl.BlockSpec((sb, tb, D_MODEL), lambda i, j: (i, j, 0))
    mod_spec = lambda chunk: pl.BlockSpec((sb, 1, D_MODEL), lambda i, j: (i, 0, chunk))
    return pl.pallas_call(
        functools.partial(_post2_kernel, sb=sb, tb=tb, ff_chunk=ff_chunk),
        out_shape=jax.ShapeDtypeStruct(x1.shape, F32),
        grid=(s_tot // sb, t_tot // tb),
        in_specs=[tile, mod_spec(4), mod_spec(3), mod_spec(5),
                  _const_spec((1, D_MODEL)), _const_spec(w_1.shape), _const_spec(w_2.shape)]
                 + [_const_spec((1, D_MODEL)) for _ in maybe_g_final],
        out_specs=tile,
        compiler_params=pltpu.CompilerParams(dimension_semantics=("arbitrary", "arbitrary"),
                                             vmem_limit_bytes=VMEM_LIMIT),
        name="post2",
    )(x1, mod, mod, mod, g_mlp, w_1, w_2, *maybe_g_final)


def _rope_table(pos):
    inv = ROPE_THETA ** (-jnp.arange(0, ROPE_DIM, 2, dtype=F32) / ROPE_DIM)
    ang = pos.astype(F32)[:, None] * inv[None, :]
    cos, sin = jnp.cos(ang), jnp.sin(ang)
    return jnp.concatenate([cos, cos, -sin, sin], axis=-1)


def _swap_halves(w):
    half = w.shape[-1] // 2
    return jnp.concatenate([w[..., half:], w[..., :half]], axis=-1)


def _layer_weights(w_in, w_q_b, w_kv_b, w_o, w_1, w_2):
    s1, s2 = Q_LORA + KV_LORA, Q_LORA + KV_LORA + ROPE_DIM
    w_kr = w_in[:, s1:s2]
    wq = w_q_b.reshape(Q_LORA, N_HEADS, NOPE_DIM + ROPE_DIM)
    wq_rope = wq[:, :, NOPE_DIM:]
    wq_pair = jnp.concatenate([wq_rope, _swap_halves(wq_rope)], axis=-1)
    wkv = w_kv_b.reshape(KV_LORA, N_HEADS, NOPE_DIM + V_DIM)
    w_uk, w_uv = wkv[:, :, :NOPE_DIM], wkv[:, :, NOPE_DIM:]
    weights = dict(
        w_a=jnp.concatenate([w_in[:, :s1], w_kr, _swap_halves(w_kr)], axis=1),
        w_bc=w_in[:, s2:],
        w_qn=wq[:, :, :NOPE_DIM].reshape(Q_LORA, N_HEADS * NOPE_DIM),
        w_qp=wq_pair.reshape(Q_LORA, N_HEADS * LANES),
        w_qpT=jnp.transpose(wq_pair, (1, 2, 0)),
        w_uk=jnp.transpose(w_uk, (1, 2, 0)),
        w_ukT=jnp.transpose(w_uk, (1, 0, 2)),
        w_uv=jnp.transpose(w_uv, (1, 0, 2)),
        w_uvT=jnp.transpose(w_uv, (1, 2, 0)),
        w_o=w_o, w_1=w_1, w_2=w_2)
    return {k: v.astype(BF16) for k, v in weights.items()}


PROMPT_TILE = 512
PROMPT_SPLIT = 2
QUERY_TILE = 512
KEY_CHUNK = 256
SAMPLE_SEQS = 64
FF_CHUNK = 1024


def kernel(x_prompt, x_sample, cache_ckv, cache_krope, state_conv, page_table, c_prompt, c_sample,
           w_ada, b_ada, g_attn, w_in, g_q, w_q_b, g_kv, w_kv_b, conv_w, w_o, g_mlp, w_1, w_2, g_final):
    depth = w_in.shape[0]
    batch, seq, _ = x_prompt.shape
    dec_b, dec_t, _ = x_sample.shape
    past_len = page_table.shape[1] * PAGE_SIZE
    cs_p = _rope_table(jnp.arange(seq, dtype=jnp.int32))
    cs_s = _rope_table(past_len + jnp.arange(dec_t, dtype=jnp.int32))
    n_c = batch + dec_b
    c_all = jnp.concatenate([c_prompt, c_sample, jnp.zeros((-n_c % 8, D_MODEL), F32)], axis=0)
    row = lambda g: g.reshape(1, -1)

    hp, hs = x_prompt, x_sample
    outs = [[] for _ in range(6)]
    for l in range(depth):
        w = _layer_weights(w_in[l], w_q_b[l], w_kv_b[l], w_o[l], w_1[l], w_2[l])
        mod = _ada(c_all, w_ada[l], row(b_ada[l]))
        mod_p = mod[:batch].reshape(batch, 1, -1)
        mod_s = mod[batch:n_c].reshape(dec_b, 1, -1)
        shared = (row(g_attn[l]), w["w_a"], row(g_q[l]), w["w_qn"], row(g_kv[l]))

        qT, ckv_p, krT_p, kcat, kT = _pre_prompt(hp, mod_p, *shared, cs_p, w["w_qpT"], w["w_ukT"], cs_p.T,
                                                 tb=PROMPT_TILE, tq=QUERY_TILE, tk=KEY_CHUNK)
        kr_p = jnp.swapaxes(krT_p, 1, 2)
        attn_p = _attn_prompt(qT, kcat, kT, w["w_uvT"], tq=QUERY_TILE, tk=KEY_CHUNK)
        buf0 = jnp.zeros((batch, CONV_W - 1, D_CONV), F32)
        x1_p, conv_p = _post1(hp, attn_p, mod_p, row(g_attn[l]), w["w_bc"], conv_w[l], w["w_o"], buf0,
                              sb=1, tb=PROMPT_TILE, n_split=PROMPT_SPLIT)

        qcat_s, ckv_s, kr_s = _pre_sample(hs, mod_s, *shared, cs_s, w["w_qp"], w["w_uk"], sb=SAMPLE_SEQS)
        attn_s = _attn_sample(page_table, qcat_s, ckv_s, kr_s, w["w_uv"], cache_ckv[l],
                              jnp.swapaxes(cache_krope[l], 1, 2))
        x1_s, conv_s = _post1(hs, attn_s, mod_s, row(g_attn[l]), w["w_bc"], conv_w[l], w["w_o"],
                              state_conv[l], sb=SAMPLE_SEQS, tb=dec_t, n_split=1)

        g_fin = (row(g_final),) if l == depth - 1 else ()
        hp = _post2(x1_p, mod_p, row(g_mlp[l]), w["w_1"], w["w_2"], g_fin, sb=1, tb=PROMPT_TILE,
                    ff_chunk=FF_CHUNK)
        hs = _post2(x1_s, mod_s, row(g_mlp[l]), w["w_1"], w["w_2"], g_fin, sb=SAMPLE_SEQS, tb=dec_t,
                    ff_chunk=FF_CHUNK)
        for acc, val in zip(outs, (ckv_p, kr_p, conv_p, ckv_s, kr_s, conv_s)):
            acc.append(val)
    return (hp, hs) + tuple(jnp.stack(o) for o in outs)
```

```python
import functools

import jax
import jax.numpy as jnp
from jax import lax
from jax.experimental import pallas as pl
from jax.experimental.pallas import tpu as pltpu

D_MODEL = 1024
N_HEADS = 8
Q_LORA = 384
KV_LORA = 256
NOPE_DIM = 128
ROPE_DIM = 64
V_DIM = D_MODEL // N_HEADS
D_CONV = D_MODEL
CONV_W = 3
D_FF = 4 * D_MODEL
PAGE_SIZE = 128
ROPE_THETA = 10000.0
EPS = 1e-6
ATTN_SCALE = (NOPE_DIM + ROPE_DIM) ** -0.5
LOG2_E = 1.4426950408889634

LANES = 128
QK_DIM = KV_LORA + LANES
NEG_BIG = -1e30
VMEM_LIMIT = 56 * 1024 * 1024

F32 = jnp.float32
BF16 = jnp.bfloat16


def _const_spec(shape):
    zeros = (0,) * len(shape)
    return pl.BlockSpec(shape, lambda *_: zeros, pipeline_mode=pl.Buffered(1))


def _dot(a, b):
    return jnp.dot(a, b, preferred_element_type=F32)


def _dot_nt(a, b):
    return lax.dot_general(a, b, (((1,), (1,)), ((), ())), preferred_element_type=F32)


def _rms(x):
    return x * lax.rsqrt(jnp.mean(x * x, axis=-1, keepdims=True) + EPS)


def _ada_kernel(c_ref, w_ref, b_ref, o_ref):
    c = c_ref[...]
    a = (c * jax.nn.sigmoid(c)).astype(BF16)
    o_ref[...] = _dot(a, w_ref[...].astype(BF16)) + b_ref[...]


def _ada(c_all, w_ada, b_ada):
    rows = c_all.shape[0]
    n_out = w_ada.shape[1]
    bn = D_MODEL
    return pl.pallas_call(
        _ada_kernel,
        out_shape=jax.ShapeDtypeStruct((rows, n_out), F32),
        grid=(n_out // bn,),
        in_specs=[pl.BlockSpec((rows, D_MODEL), lambda j: (0, 0)),
                  pl.BlockSpec((D_MODEL, bn), lambda j: (0, j)),
                  pl.BlockSpec((1, bn), lambda j: (0, j))],
        out_specs=pl.BlockSpec((rows, bn), lambda j: (0, j)),
        compiler_params=pltpu.CompilerParams(dimension_semantics=("arbitrary",),
                                             vmem_limit_bytes=VMEM_LIMIT),
        name="ada",
    )(c_all, w_ada, b_ada)


def _modulated(x_ref, sc_ref, sh_ref, g_ref):
    u = _rms(x_ref[...]) * g_ref[...] * (1.0 + sc_ref[...]) + sh_ref[...]
    sb, tb, d = u.shape
    return u.reshape(sb * tb, d).astype(BF16)


def _rope_pair(chunk, cs):
    t = chunk * cs
    return t + pltpu.roll(t, LANES // 2, axis=2)


def _latent_and_query(x_ref, sc_ref, sh_ref, gattn_ref, wa_ref, gq_ref, wqn_ref, gkv_ref, cs_ref, ckv_ref):
    sb, tb, _ = x_ref.shape
    u = _modulated(x_ref, sc_ref, sh_ref, gattn_ref)
    pa = _dot(u, wa_ref[...])
    ckv = _rms(pa[:, Q_LORA:Q_LORA + KV_LORA]) * gkv_ref[...]
    ckv_ref[...] = ckv.reshape(sb, tb, KV_LORA)
    kr2 = _rope_pair(pa[:, Q_LORA + KV_LORA:].reshape(sb, tb, LANES), cs_ref[...])
    qn = (_rms(pa[:, :Q_LORA]) * gq_ref[...]).astype(BF16)
    q_nope = _dot(qn, wqn_ref[...])
    return ckv, kr2, qn, q_nope


def _pre_sample_kernel(x_ref, sc_ref, sh_ref, gattn_ref, wa_ref, gq_ref, wqn_ref, gkv_ref, cs_ref,
                       wqp_ref, wuk_ref, qcat_ref, ckv_ref, kr_ref):
    sb, tb, _ = x_ref.shape
    _, kr2, qn, q_nope = _latent_and_query(x_ref, sc_ref, sh_ref, gattn_ref, wa_ref, gq_ref, wqn_ref,
                                           gkv_ref, cs_ref, ckv_ref)
    kr_ref[...] = kr2[:, :, :ROPE_DIM]
    q_pair = _dot(qn, wqp_ref[...])
    for h in range(N_HEADS):
        q_lat = _dot(q_nope[:, h * NOPE_DIM:(h + 1) * NOPE_DIM].astype(BF16), wuk_ref[h]) * ATTN_SCALE
        q_rope = _rope_pair(q_pair[:, h * LANES:(h + 1) * LANES].reshape(sb, tb, LANES), cs_ref[...])
        qcat_ref[h, :, :, 0:KV_LORA] = q_lat.reshape(sb, tb, KV_LORA)
        qcat_ref[h, :, :, KV_LORA:QK_DIM] = q_rope * ATTN_SCALE


def _pre_prompt_kernel(x_ref, sc_ref, sh_ref, gattn_ref, wa_ref, gq_ref, wqn_ref, gkv_ref, cs_ref,
                       wqpT_ref, wukT_ref, csT_ref, qT_ref, ckv_ref, krT_ref, kcat_ref, kT_ref, *, tq, tk):
    _, tb, _ = x_ref.shape
    ckv, kr2, qn, q_nope = _latent_and_query(x_ref, sc_ref, sh_ref, gattn_ref, wa_ref, gq_ref, wqn_ref,
                                             gkv_ref, cs_ref, ckv_ref)
    krT_ref[0] = kr2.reshape(tb, LANES).T[:ROPE_DIM]
    lane = lax.broadcasted_iota(jnp.int32, kr2.shape, 2)
    kcat_ref[:, :, 0:KV_LORA] = ckv.reshape(1, tb, KV_LORA).astype(BF16)
    kcat_ref[:, :, KV_LORA:QK_DIM] = jnp.where(lane < ROPE_DIM, kr2, 0.0).astype(BF16)
    ckvT = ckv.T.astype(BF16)
    csT = csT_ref[...]
    scale = ATTN_SCALE * LOG2_E
    for a in range(tb // tk):
        kT_ref[0, a] = ckvT[:, a * tk:(a + 1) * tk]
    for h in range(N_HEADS):
        q_latT = _dot_nt(wukT_ref[h], q_nope[:, h * NOPE_DIM:(h + 1) * NOPE_DIM].astype(BF16))
        t = _dot_nt(wqpT_ref[h], qn) * csT
        q_ropeT = t + jnp.concatenate([t[LANES // 2:], t[:LANES // 2]], axis=0)
        for a in range(tb // tq):
            cols = slice(a * tq, (a + 1) * tq)
            qT_ref[0, a, 0:KV_LORA, h * tq:(h + 1) * tq] = (q_latT[:, cols] * scale).astype(BF16)
            qT_ref[0, a, KV_LORA:QK_DIM, h * tq:(h + 1) * tq] = (q_ropeT[:, cols] * scale).astype(BF16)


def _pre_common_specs(sb, tb, w_a, w_qn):
    tile = lambda w: pl.BlockSpec((sb, tb, w), lambda i, j: (i, j, 0))
    mod_spec = lambda chunk: pl.BlockSpec((sb, 1, D_MODEL), lambda i, j: (i, 0, chunk))
    in_specs = [tile(D_MODEL), mod_spec(1), mod_spec(0),
                _const_spec((1, D_MODEL)), _const_spec(w_a.shape), _const_spec((1, Q_LORA)),
                _const_spec(w_qn.shape), _const_spec((1, KV_LORA)),
                pl.BlockSpec((tb, LANES), lambda i, j: (j, 0))]
    return tile, in_specs


def _pre_sample(x, mod, g_attn, w_a, g_q, w_qn, g_kv, cs, w_qp, w_uk, *, sb):
    s_tot, tb, _ = x.shape
    tile, in_specs = _pre_common_specs(sb, tb, w_a, w_qn)
    return pl.pallas_call(
        _pre_sample_kernel,
        out_shape=[jax.ShapeDtypeStruct((N_HEADS, s_tot, tb, QK_DIM), F32),
                   jax.ShapeDtypeStruct((s_tot, tb, KV_LORA), F32),
                   jax.ShapeDtypeStruct((s_tot, tb, ROPE_DIM), F32)],
        grid=(s_tot // sb, 1),
        in_specs=in_specs + [_const_spec(w_qp.shape), _const_spec(w_uk.shape)],
        out_specs=[pl.BlockSpec((N_HEADS, sb, tb, QK_DIM), lambda i, j: (0, i, j, 0)),
                   tile(KV_LORA), tile(ROPE_DIM)],
        compiler_params=pltpu.CompilerParams(dimension_semantics=("arbitrary", "arbitrary"),
                                             vmem_limit_bytes=VMEM_LIMIT),
        name="pre_sample",
    )(x, mod, mod, g_attn, w_a, g_q, w_qn, g_kv, cs, w_qp, w_uk)


def _pre_prompt(x, mod, g_attn, w_a, g_q, w_qn, g_kv, cs, w_qpT, w_ukT, csT, *, tb, tq, tk):
    b, t_tot, _ = x.shape
    assert tb % tq == 0 and tb % tk == 0
    tile, in_specs = _pre_common_specs(1, tb, w_a, w_qn)
    return pl.pallas_call(
        functools.partial(_pre_prompt_kernel, tq=tq, tk=tk),
        out_shape=[jax.ShapeDtypeStruct((b, t_tot // tq, QK_DIM, N_HEADS * tq), BF16),
                   jax.ShapeDtypeStruct((b, t_tot, KV_LORA), F32),
                   jax.ShapeDtypeStruct((b, ROPE_DIM, t_tot), F32),
                   jax.ShapeDtypeStruct((b, t_tot, QK_DIM), BF16),
                   jax.ShapeDtypeStruct((b, t_tot // tk, KV_LORA, tk), BF16)],
        grid=(b, t_tot // tb),
        in_specs=in_specs + [_const_spec(w_qpT.shape), _const_spec(w_ukT.shape),
                             pl.BlockSpec((LANES, tb), lambda i, j: (0, j))],
        out_specs=[pl.BlockSpec((1, tb // tq, QK_DIM, N_HEADS * tq), lambda i, j: (i, j, 0, 0)),
                   tile(KV_LORA), pl.BlockSpec((1, ROPE_DIM, tb), lambda i, j: (i, 0, j)), tile(QK_DIM),
                   pl.BlockSpec((1, tb // tk, KV_LORA, tk), lambda i, j: (i, j, 0, 0))],
        compiler_params=pltpu.CompilerParams(dimension_semantics=("arbitrary", "arbitrary"),
                                             vmem_limit_bytes=VMEM_LIMIT),
        name="pre_prompt",
    )(x, mod, mod, g_attn, w_a, g_q, w_qn, g_kv, cs, w_qpT, w_ukT, csT)


BLOCK_LOOKAHEAD = 3


def _attn_prompt_kernel(qT_ref, k_ref, kT_ref, wuvT_ref, o_ref, m_ref, l_ref, acc_ref, ahead_ref, *,
                        tq, tk, lookahead):
    i = pl.program_id(1)
    per = tq // tk
    m_ref[...] = jnp.full(m_ref.shape, NEG_BIG, F32)
    l_ref[...] = jnp.zeros(l_ref.shape, F32)
    acc_ref[...] = jnp.zeros(acc_ref.shape, F32)
    k_tok = lax.broadcasted_iota(jnp.int32, (tk, tk), 0)
    q_tok = lax.broadcasted_iota(jnp.int32, (tk, tk), 1)
    all_blocks = [(h, e) for h in range(N_HEADS) for e in range(per)]

    def cols(block):
        h, e = block
        return slice(h * tq + e * tk, h * tq + (e + 1) * tk)

    def scores(j, block):
        keys = k_ref[0, pl.ds(pl.multiple_of(j * tk, tk), tk), :]
        return _dot(keys, qT_ref[0, 0, :, cols(block)])

    def run(steps, after):
        pending = [ahead_ref[n] for n in range(lookahead)]
        for n, (j, block, on_diagonal) in enumerate(steps):
            ahead = n + lookahead
            if ahead < len(steps):
                pending.append(scores(*steps[ahead][:2]))
            elif ahead - len(steps) < len(after):
                ahead_ref[ahead - len(steps)] = scores(*after[ahead - len(steps)])
            sT = pending.pop(0)
            if on_diagonal:
                sT = jnp.where(k_tok <= q_tok, sT, NEG_BIG)
            c = cols(block)
            m_prev = m_ref[:, c]
            m_new = jnp.maximum(m_prev, jnp.max(sT, axis=0, keepdims=True))
            alpha = jnp.exp2(m_prev - m_new)
            pT = jnp.exp2(sT - m_new)
            l_ref[:, c] = alpha * l_ref[:, c] + jnp.sum(pT, axis=0, keepdims=True)
            acc_ref[:, c] = alpha * acc_ref[:, c] + _dot(kT_ref[0, j], pT.astype(BF16))
            m_ref[:, c] = m_new

    for n in range(lookahead):
        ahead_ref[n] = scores(0, all_blocks[n])

    def full_chunks(jj, carry):
        first = jj * per
        run([(first + c, block, False) for c in range(per) for block in all_blocks],
            [(first + per, block) for block in all_blocks[:lookahead]])
        return carry

    lax.fori_loop(0, i, full_chunks, 0)

    run([(i * per + c, (h, e), e == c) for c in range(per) for (h, e) in all_blocks if e >= c], [])

    for h in range(N_HEADS):
        head = slice(h * tq, (h + 1) * tq)
        oT = (acc_ref[:, head] * (1.0 / l_ref[:, head])).astype(BF16)
        o_ref[0, :, h * V_DIM:(h + 1) * V_DIM] = _dot(wuvT_ref[h], oT).T


def _attn_prompt(qT, kcat, kT, w_uvT, *, tq, tk):
    b, n_q, _, rows = qT.shape
    assert rows == N_HEADS * tq and tq % tk == 0 and BLOCK_LOOKAHEAD <= tq // tk * N_HEADS
    t = n_q * tq
    return pl.pallas_call(
        functools.partial(_attn_prompt_kernel, tq=tq, tk=tk, lookahead=BLOCK_LOOKAHEAD),
        out_shape=jax.ShapeDtypeStruct((b, t, N_HEADS * V_DIM), F32),
        grid=(b, n_q),
        in_specs=[pl.BlockSpec((1, 1, QK_DIM, rows), lambda bi, i: (bi, i, 0, 0)),
                  pl.BlockSpec((1, t, QK_DIM), lambda bi, i: (bi, 0, 0)),
                  pl.BlockSpec((1, t // tk, KV_LORA, tk), lambda bi, i: (bi, 0, 0, 0)),
                  _const_spec(w_uvT.shape)],
        out_specs=pl.BlockSpec((1, tq, N_HEADS * V_DIM), lambda bi, i: (bi, i, 0)),
        scratch_shapes=[pltpu.VMEM((1, rows), F32), pltpu.VMEM((1, rows), F32),
                        pltpu.VMEM((KV_LORA, rows), F32), pltpu.VMEM((BLOCK_LOOKAHEAD, tk, tk), F32)],
        compiler_params=pltpu.CompilerParams(dimension_semantics=("arbitrary", "arbitrary"),
                                             vmem_limit_bytes=VMEM_LIMIT),
        name="attn_prompt",
    )(qT, kcat, kT, w_uvT)


DMA_UNROLL = 8
SAMPLE_KEY_BLOCKS = 16
SAMPLE_LOOKAHEAD = 3
SEQS_PER_STEP = 4
FETCH_AHEAD = 3


def _attn_sample_kernel(pt_ref, q_ref, nk_ref, nr_ref, wuv_ref, ckv_hbm, krT_hbm, o_ref,
                        kbuf, rbuf, nkbuf, nrbuf, sem, *, n_pages, dec_t, n_blocks, lookahead):
    g = pl.program_id(0)
    n_steps = pl.num_programs(0)
    n_seq = n_steps * SEQS_PER_STEP
    rows = N_HEADS * dec_t
    past = n_pages * PAGE_SIZE

    def page_copies(page, p, slot):
        pos = pl.ds(pl.multiple_of(p * PAGE_SIZE, PAGE_SIZE), PAGE_SIZE)
        return (pltpu.make_async_copy(ckv_hbm.at[page], kbuf.at[slot, pos], sem.at[0, slot]),
                pltpu.make_async_copy(krT_hbm.at[page], rbuf.at[slot, :, pos], sem.at[1, slot]))

    def start_fetch(seq, slot):
        for p in range(n_pages):
            for cp in page_copies(pt_ref[seq, p], p, slot):
                cp.start(priority=p % 2)

    def wait_fetch(slot):
        def body(p, carry):
            for cp in page_copies(0, p, slot):
                cp.wait()
            return carry
        lax.fori_loop(0, n_pages, body, 0, unroll=DMA_UNROLL)

    def partial_softmax(s, k_lat):
        m_b = jnp.max(s, axis=-1, keepdims=True)
        p = jnp.exp(s - m_b)
        return m_b, jnp.sum(p, axis=-1, keepdims=True), _dot(p.astype(BF16), k_lat)

    def attend(slot):
        q = q_ref[:, slot].reshape(rows, QK_DIM)
        q_lat = q[:, :KV_LORA].astype(BF16)
        q_rope = q[:, KV_LORA:KV_LORA + ROPE_DIM].astype(BF16)

        def block_scores(b):
            ks = slice(b * past // n_blocks, (b + 1) * past // n_blocks)
            k_lat = kbuf[slot, ks, :].astype(BF16)
            return _dot_nt(q_lat, k_lat) + _dot(q_rope, rbuf[slot, :, ks].astype(BF16)), k_lat

        pending = [block_scores(b) for b in range(lookahead)]
        parts = []
        for b in range(n_blocks):
            if b + lookahead < n_blocks:
                pending.append(block_scores(b + lookahead))
            parts.append(partial_softmax(*pending.pop(0)))

        nkbuf[slot, 0:dec_t, :] = nk_ref[slot]
        nrbuf[slot, 0:dec_t, :] = nr_ref[slot]
        n_lat = nkbuf[slot].astype(BF16)
        s_new = _dot_nt(q_lat, n_lat) + _dot_nt(q_rope, nrbuf[slot].astype(BF16))
        q_tok = lax.broadcasted_iota(jnp.int32, (N_HEADS, dec_t, PAGE_SIZE), 1).reshape(rows, PAGE_SIZE)
        k_tok = lax.broadcasted_iota(jnp.int32, (rows, PAGE_SIZE), 1)
        s_new = jnp.where(k_tok <= q_tok, s_new, NEG_BIG)
        parts.append(partial_softmax(s_new, n_lat))

        m = functools.reduce(jnp.maximum, [m_b for m_b, _, _ in parts])
        weights = [jnp.exp(m_b - m) for m_b, _, _ in parts]
        l = functools.reduce(jnp.add, [w_b * l_b for w_b, (_, l_b, _) in zip(weights, parts)])
        acc = functools.reduce(jnp.add, [w_b * acc_b for w_b, (_, _, acc_b) in zip(weights, parts)])

        o = (acc / l).astype(BF16)
        for h in range(N_HEADS):
            o_ref[slot, :, h * V_DIM:(h + 1) * V_DIM] = _dot(o[h * dec_t:(h + 1) * dec_t], wuv_ref[h])

    first = g * SEQS_PER_STEP

    @pl.when(g == 0)
    def _():
        for a in range(FETCH_AHEAD):
            start_fetch(a, a)
        nkbuf[...] = jnp.zeros(nkbuf.shape, F32)
        nrbuf[...] = jnp.zeros(nrbuf.shape, F32)

    for a in range(SEQS_PER_STEP):
        wait_fetch(a)
        start_fetch(jnp.minimum(first + a + FETCH_AHEAD, n_seq - 1), (a + FETCH_AHEAD) % SEQS_PER_STEP)
        attend(a)

    @pl.when(g == n_steps - 1)
    def _():
        for a in range(FETCH_AHEAD):
            wait_fetch(a)


def _attn_sample(page_table, qcat, new_ckv, new_kr, w_uv, cache_ckv, cache_krT):
    _, n_seq, dec_t, _ = qcat.shape
    n_pages = page_table.shape[1]
    past = n_pages * PAGE_SIZE
    per = SEQS_PER_STEP
    assert n_seq % per == 0 and 0 < FETCH_AHEAD < per
    grid_spec = pltpu.PrefetchScalarGridSpec(
        num_scalar_prefetch=1,
        grid=(n_seq // per,),
        in_specs=[pl.BlockSpec((N_HEADS, per, dec_t, QK_DIM), lambda g, pt: (0, g, 0, 0)),
                  pl.BlockSpec((per, dec_t, KV_LORA), lambda g, pt: (g, 0, 0)),
                  pl.BlockSpec((per, dec_t, ROPE_DIM), lambda g, pt: (g, 0, 0)),
                  pl.BlockSpec(w_uv.shape, lambda g, pt: (0, 0, 0), pipeline_mode=pl.Buffered(1)),
                  pl.BlockSpec(memory_space=pl.ANY),
                  pl.BlockSpec(memory_space=pl.ANY)],
        out_specs=pl.BlockSpec((per, dec_t, N_HEADS * V_DIM), lambda g, pt: (g, 0, 0)),
        scratch_shapes=[pltpu.VMEM((per, past, KV_LORA), F32),
                        pltpu.VMEM((per, ROPE_DIM, past), F32),
                        pltpu.VMEM((per, PAGE_SIZE, KV_LORA), F32),
                        pltpu.VMEM((per, PAGE_SIZE, ROPE_DIM), F32),
                        pltpu.SemaphoreType.DMA((2, per))],
    )
    return pl.pallas_call(
        functools.partial(_attn_sample_kernel, n_pages=n_pages, dec_t=dec_t, n_blocks=SAMPLE_KEY_BLOCKS,
                          lookahead=SAMPLE_LOOKAHEAD),
        out_shape=jax.ShapeDtypeStruct((n_seq, dec_t, N_HEADS * V_DIM), F32),
        grid_spec=grid_spec,
        compiler_params=pltpu.CompilerParams(dimension_semantics=("arbitrary",),
                                             vmem_limit_bytes=VMEM_LIMIT),
        name="attn_sample",
    )(page_table, qcat, new_ckv, new_kr, w_uv, cache_ckv, cache_krT)


def _post1_kernel(x_ref, attn_ref, sc_ref, sh_ref, g1_ref, gattn_ref, wbc_ref, cw_ref, wo_ref,
                  prev_ref, x1_ref, nconv_ref, carry_ref, *, sb, tb, n_split):
    ts = tb // n_split
    m = sb * ts

    @pl.when(pl.program_id(1) == 0)
    def _():
        carry_ref[...] = prev_ref[...]

    def project(rows):
        x = x_ref[:, rows, :]
        u = _rms(x) * gattn_ref[...] * (1.0 + sc_ref[...]) + sh_ref[...]
        u = u.reshape(m, D_MODEL).astype(BF16)
        return x, _dot(u, wbc_ref[:, :3 * D_CONV]), _dot(u, wbc_ref[:, 3 * D_CONV:])

    def finish(rows, x, pb, pc):
        v = (pb[:, D_CONV:2 * D_CONV] * pb[:, 2 * D_CONV:]).reshape(sb, ts, D_CONV)
        p0 = carry_ref[:, 0:1, :]
        p1 = carry_ref[:, 1:2, :]
        t = lax.broadcasted_iota(jnp.int32, v.shape, 1)
        r1 = pltpu.roll(v, 1, axis=1)
        r2 = pltpu.roll(v, 2, axis=1)
        v1 = jnp.where(t >= 1, r1, p1)
        v2 = jnp.where(t >= 2, r2, jnp.where(t == 1, p1, p0))
        cw = cw_ref[...]
        z = cw[0:1, :] * v2 + cw[1:2, :] * v1 + cw[2:3, :] * v
        conv_out = pb[:, :D_CONV] * z.reshape(m, D_CONV)
        carry_ref[...] = r2[:, 0:CONV_W - 1, :]

        g = jax.nn.sigmoid(pc)
        merged = g[:, :D_MODEL] * attn_ref[:, rows, :].reshape(m, D_MODEL) + g[:, D_MODEL:] * conv_out
        proj = _dot(merged.astype(BF16), wo_ref[...]).reshape(sb, ts, D_MODEL)
        x1_ref[:, rows, :] = x + g1_ref[...] * proj

    ranges = [slice(r * ts, (r + 1) * ts) for r in range(n_split)]
    projected = [project(rows) for rows in ranges]
    for rows, args in zip(ranges, projected):
        finish(rows, *args)
    nconv_ref[...] = carry_ref[...]


def _post1(x, attn, mod, g_attn, w_bc, conv_w, w_o, prev, *, sb, tb, n_split):
    s_tot, t_tot, _ = x.shape
    tile = pl.BlockSpec((sb, tb, D_MODEL), lambda i, j: (i, j, 0))
    mod_spec = lambda chunk: pl.BlockSpec((sb, 1, D_MODEL), lambda i, j: (i, 0, chunk))
    state = pl.BlockSpec((sb, CONV_W - 1, D_CONV), lambda i, j: (i, 0, 0))
    return pl.pallas_call(
        functools.partial(_post1_kernel, sb=sb, tb=tb, n_split=n_split),
        out_shape=[jax.ShapeDtypeStruct(x.shape, F32),
                   jax.ShapeDtypeStruct((s_tot, CONV_W - 1, D_CONV), F32)],
        grid=(s_tot // sb, t_tot // tb),
        in_specs=[tile, tile, mod_spec(1), mod_spec(0), mod_spec(2),
                  _const_spec((1, D_MODEL)), _const_spec(w_bc.shape),
                  _const_spec(conv_w.shape), _const_spec(w_o.shape), state],
        out_specs=[tile, state],
        scratch_shapes=[pltpu.VMEM((sb, CONV_W - 1, D_CONV), F32)],
        compiler_params=pltpu.CompilerParams(dimension_semantics=("arbitrary", "arbitrary"),
                                             vmem_limit_bytes=VMEM_LIMIT),
        name="post1",
    )(x, attn, mod, mod, mod, g_attn, w_bc, conv_w, w_o, prev)


def _post2_kernel(x1_ref, sc_ref, sh_ref, g2_ref, gmlp_ref, w1_ref, w2_ref, *rest, sb, tb, ff_chunk):
    y_ref = rest[-1]
    m = sb * tb
    u = _modulated(x1_ref, sc_ref, sh_ref, gmlp_ref)
    acc = jnp.zeros((m, D_MODEL), F32)
    for c in range(D_FF // ff_chunk):
        cols = slice(c * ff_chunk, (c + 1) * ff_chunk)
        hdn = jnp.square(jnp.maximum(_dot(u, w1_ref[:, cols]), 0.0))
        acc = acc + _dot(hdn.astype(BF16), w2_ref[cols, :])
    x2 = x1_ref[...] + g2_ref[...] * acc.reshape(sb, tb, D_MODEL)
    y_ref[...] = _rms(x2) * rest[0][...] if len(rest) == 2 else x2


def _post2(x1, mod, g_mlp, w_1, w_2, maybe_g_final, *, sb, tb, ff_chunk):
    s_tot, t_tot, _ = x1.shape
    tile = pl.BlockSpec((sb, tb, D_MODEL), lambda i, j: (i, j, 0))
    mod_spec = lambda chunk: pl.BlockSpec((sb, 1, D_MODEL), lambda i, j: (i, 0, chunk))
    return pl.pallas_call(
        functools.partial(_post2_kernel, sb=sb, tb=tb, ff_chunk=ff_chunk),
        out_shape=jax.ShapeDtypeStruct(x1.shape, F32),
        grid=(s_tot // sb, t_tot // tb),
        in_specs=[tile, mod_spec(4), mod_spec(3), mod_spec(5),
                  _const_spec((1, D_MODEL)), _const_spec(w_1.shape), _const_spec(w_2.shape)]
                 + [_const_spec((1, D_MODEL)) for _ in maybe_g_final],
        out_specs=tile,
        compiler_params=pltpu.CompilerParams(dimension_semantics=("arbitrary", "arbitrary"),
                                             vmem_limit_bytes=VMEM_LIMIT),
        name="post2",
    )(x1, mod, mod, mod, g_mlp, w_1, w_2, *maybe_g_final)


def _rope_table(pos):
    inv = ROPE_THETA ** (-jnp.arange(0, ROPE_DIM, 2, dtype=F32) / ROPE_DIM)
    ang = pos.astype(F32)[:, None] * inv[None, :]
    cos, sin = jnp.cos(ang), jnp.sin(ang)
    return jnp.concatenate([cos, cos, -sin, sin], axis=-1)


def _swap_halves(w):
    half = w.shape[-1] // 2
    return jnp.concatenate([w[..., half:], w[..., :half]], axis=-1)


def _layer_weights(w_in, w_q_b, w_kv_b, w_o, w_1, w_2):
    s1, s2 = Q_LORA + KV_LORA, Q_LORA + KV_LORA + ROPE_DIM
    w_kr = w_in[:, s1:s2]
    wq = w_q_b.reshape(Q_LORA, N_HEADS, NOPE_DIM + ROPE_DIM)
    wq_rope = wq[:, :, NOPE_DIM:]
    wq_pair = jnp.concatenate([wq_rope, _swap_halves(wq_rope)], axis=-1)
    wkv = w_kv_b.reshape(KV_LORA, N_HEADS, NOPE_DIM + V_DIM)
    w_uk, w_uv = wkv[:, :, :NOPE_DIM], wkv[:, :, NOPE_DIM:]
    weights = dict(
        w_a=jnp.concatenate([w_in[:, :s1], w_kr, _swap_halves(w_kr)], axis=1),
        w_bc=w_in[:, s2:],
        w_qn=wq[:, :, :NOPE_DIM].reshape(Q_LORA, N_HEADS * NOPE_DIM),
        w_qp=wq_pair.reshape(Q_LORA, N_HEADS * LANES),
        w_qpT=jnp.transpose(wq_pair, (1, 2, 0)),
        w_uk=jnp.transpose(w_uk, (1, 2, 0)),
        w_ukT=jnp.transpose(w_uk, (1, 0, 2)),
        w_uv=jnp.transpose(w_uv, (1, 0, 2)),
        w_uvT=jnp.transpose(w_uv, (1, 2, 0)),
        w_o=w_o, w_1=w_1, w_2=w_2)
    return {k: v.astype(BF16) for k, v in weights.items()}


PROMPT_TILE = 512
PROMPT_SPLIT = 2
QUERY_TILE = 512
KEY_CHUNK = 256
SAMPLE_SEQS = 64
FF_CHUNK = 1024


def kernel(x_prompt, x_sample, cache_ckv, cache_krope, state_conv, page_table, c_prompt, c_sample,
           w_ada, b_ada, g_attn, w_in, g_q, w_q_b, g_kv, w_kv_b, conv_w, w_o, g_mlp, w_1, w_2, g_final):
    depth = w_in.shape[0]
    batch, seq, _ = x_prompt.shape
    dec_b, dec_t, _ = x_sample.shape
    past_len = page_table.shape[1] * PAGE_SIZE
    cs_p = _rope_table(jnp.arange(seq, dtype=jnp.int32))
    cs_s = _rope_table(past_len + jnp.arange(dec_t, dtype=jnp.int32))
    n_c = batch + dec_b
    c_all = jnp.concatenate([c_prompt, c_sample, jnp.zeros((-n_c % 8, D_MODEL), F32)], axis=0)
    row = lambda g: g.reshape(1, -1)

    hp, hs = x_prompt, x_sample
    outs = [[] for _ in range(6)]
    for l in range(depth):
        w = _layer_weights(w_in[l], w_q_b[l], w_kv_b[l], w_o[l], w_1[l], w_2[l])
        mod = _ada(c_all, w_ada[l], row(b_ada[l]))
        mod_p = mod[:batch].reshape(batch, 1, -1)
        mod_s = mod[batch:n_c].reshape(dec_b, 1, -1)
        shared = (row(g_attn[l]), w["w_a"], row(g_q[l]), w["w_qn"], row(g_kv[l]))

        qT, ckv_p, krT_p, kcat, kT = _pre_prompt(hp, mod_p, *shared, cs_p, w["w_qpT"], w["w_ukT"], cs_p.T,
                                                 tb=PROMPT_TILE, tq=QUERY_TILE, tk=KEY_CHUNK)
        kr_p = jnp.swapaxes(krT_p, 1, 2)
        attn_p = _attn_prompt(qT, kcat, kT, w["w_uvT"], tq=QUERY_TILE, tk=KEY_CHUNK)
        buf0 = jnp.zeros((batch, CONV_W - 1, D_CONV), F32)
        x1_p, conv_p = _post1(hp, attn_p, mod_p, row(g_attn[l]), w["w_bc"], conv_w[l], w["w_o"], buf0,
                              sb=1, tb=PROMPT_TILE, n_split=PROMPT_SPLIT)

        qcat_s, ckv_s, kr_s = _pre_sample(hs, mod_s, *shared, cs_s, w["w_qp"], w["w_uk"], sb=SAMPLE_SEQS)
        attn_s = _attn_sample(page_table, qcat_s, ckv_s, kr_s, w["w_uv"], cache_ckv[l],
                              jnp.swapaxes(cache_krope[l], 1, 2))
        x1_s, conv_s = _post1(hs, attn_s, mod_s, row(g_attn[l]), w["w_bc"], conv_w[l], w["w_o"],
                              state_conv[l], sb=SAMPLE_SEQS, tb=dec_t, n_split=1)

        g_fin = (row(g_final),) if l == depth - 1 else ()
        hp = _post2(x1_p, mod_p, row(g_mlp[l]), w["w_1"], w["w_2"], g_fin, sb=1, tb=PROMPT_TILE,
                    ff_chunk=FF_CHUNK)
        hs = _post2(x1_s, mod_s, row(g_mlp[l]), w["w_1"], w["w_2"], g_fin, sb=SAMPLE_SEQS, tb=dec_t,
                    ff_chunk=FF_CHUNK)
        for acc, val in zip(outs, (ckv_p, kr_p, conv_p, ckv_s, kr_s, conv_s)):
            acc.append(val)
    return (hp, hs) + tuple(jnp.stack(o) for o in outs)
```
